```python
import math
import jax, jax.numpy as jnp
from jax import lax
import numpy as np

D_MODEL = 1024
BATCH = 8
SEQ = 4096
DEPTH = 1

GRID_W = 64
CTX_LEN = 256
EPS = 1e-6

SSM_WIDTH = 512
SSM_GROUP = 16
SSM_GROUPS = SSM_WIDTH // SSM_GROUP
SSM_STATE = 64

NA_HEADS = 8
NA_HEAD_DIM = 64
NA_WIDTH = NA_HEADS * NA_HEAD_DIM
NA_ROWS_MAX = 8
NA_COLS = 16

COL_K = SSM_WIDTH
COL_V = COL_K + NA_WIDTH
COL_Q = COL_V + NA_WIDTH
COL_GA = COL_Q + NA_WIDTH
COL_GB = COL_GA + D_MODEL
IN_COLS = COL_GB + D_MODEL
CTX_COLS = COL_Q

PEER_HEADS = 8
PEER_KEYS = 128
PEER_EXPERTS = PEER_KEYS * PEER_KEYS
PEER_QDIM = 256
PEER_HALF = PEER_QDIM // 2
PEER_TOPK = 16
PEER_BLOCK = 128

kernel_name = 'hybrid_s5_natten_peer_dit_layer'


def rms_norm(t, g):
    tf = t.astype(jnp.float32)
    y = tf * lax.rsqrt(jnp.mean(tf * tf, axis=-1, keepdims=True) + EPS)
    return (y * g.astype(jnp.float32)).astype(t.dtype)


def modulate(t, g, shift, scale):
    return rms_norm(t, g) * (1 + scale) + shift


def adaln(cond, w_mod, b_mod, n_chunks):
    m = jax.nn.silu(cond) @ w_mod[:, :n_chunks * D_MODEL] + b_mod[:n_chunks * D_MODEL]
    return jnp.split(m, n_chunks, axis=-1)


def s5_discretise(p, direction):
    f32 = jnp.float32
    lam = lax.complex(p['ssm_a_re'][direction].astype(f32), p['ssm_a_im'][direction].astype(f32))
    dt = jnp.exp(p['ssm_log_dt'][direction].astype(f32))[:, None]
    lam_bar = jnp.exp(lam * dt)
    b = lax.complex(p['ssm_b_re'][direction].astype(f32), p['ssm_b_im'][direction].astype(f32))
    b_bar = ((lam_bar - 1) / lam)[..., None] * b
    cmat = lax.complex(p['ssm_c_re'][direction].astype(f32), p['ssm_c_im'][direction].astype(f32))
    return lam_bar, b_bar, cmat


def _linear_recurrence(e1, e2):
    a1, b1 = e1
    a2, b2 = e2
    return a2 * a1, a2 * b1 + b2


def s5_scan(lam_bar, bu, s0, reverse):
    if s0 is not None:
        bu = bu.at[-1 if reverse else 0].add(lam_bar * s0)
    a = jnp.broadcast_to(lam_bar, (bu.shape[0], 1) + lam_bar.shape)
    _, s = lax.associative_scan(_linear_recurrence, (a, bu), reverse=reverse, axis=0)
    return s


def s5_mixer(u_lat, u_ctx, p, with_ctx_out):
    def groups(u):
        n, b = u.shape[1], u.shape[0]
        return jnp.swapaxes(u.astype(jnp.float32), 0, 1).reshape(n, b, SSM_GROUPS, SSM_GROUP).astype(jnp.complex64)

    def readout(cmat, s):
        y = jnp.einsum('ghp,nbgp->nbgh', cmat, s).real
        return jnp.swapaxes(y.reshape(y.shape[0], y.shape[1], SSM_WIDTH), 0, 1)

    ul, uc = groups(u_lat), groups(u_ctx)
    d_skip = p['ssm_d'].astype(jnp.float32)
    y_lat = d_skip * u_lat.astype(jnp.float32)
    y_ctx = d_skip * u_ctx.astype(jnp.float32) if with_ctx_out else None
    for direction, reverse in ((0, False), (1, True)):
        lam_bar, b_bar, cmat = s5_discretise(p, direction)
        s_ctx = s5_scan(lam_bar, jnp.einsum('gph,nbgh->nbgp', b_bar, uc), None, reverse)
        s0 = s_ctx[0] if reverse else s_ctx[-1]
        s_lat = s5_scan(lam_bar, jnp.einsum('gph,nbgh->nbgp', b_bar, ul), s0, reverse)
        y_lat = y_lat + readout(cmat, s_lat)
        if with_ctx_out:
            y_ctx = y_ctx + readout(cmat, s_ctx)
    return y_lat, y_ctx


def s5_glu(y, p):
    z = jax.nn.gelu(y)
    return z * jax.nn.sigmoid(z @ p['w_glu'] + p['b_glu'])


def ctx_heads(t):
    b, n, _ = t.shape
    return t.reshape(b, n, NA_HEADS, NA_HEAD_DIM).transpose(0, 2, 1, 3)


def na_mixer(q, k, v, k_ctx, v_ctx, rpb):
    b, l, _ = q.shape
    rows = l // GRID_W
    kh = min(NA_ROWS_MAX, rows)
    n_band = kh * GRID_W

    def grid_heads(t):
        return t.reshape(b, rows, GRID_W, NA_HEADS, NA_HEAD_DIM).transpose(0, 3, 1, 2, 4)

    qg, kg, vg = grid_heads(q), grid_heads(k), grid_heads(v)
    cols = jnp.arange(GRID_W)
    c0 = jnp.clip(cols - NA_COLS // 2, 0, GRID_W - NA_COLS)
    col_in = (cols[None, :] >= c0[:, None]) & (cols[None, :] < c0[:, None] + NA_COLS)
    dc = jnp.clip(cols[None, :] - cols[:, None] + NA_COLS - 1, 0, 2 * NA_COLS - 2)
    rpb = rpb.astype(jnp.float32)
    scale = NA_HEAD_DIM ** -0.5

    def row_block(args):
        r, q_row = args
        r0 = jnp.clip(r - kh // 2, 0, rows - kh)
        k_band = lax.dynamic_slice_in_dim(kg, r0, kh, axis=2).reshape(b, NA_HEADS, n_band, NA_HEAD_DIM)
        v_band = lax.dynamic_slice_in_dim(vg, r0, kh, axis=2).reshape(b, NA_HEADS, n_band, NA_HEAD_DIM)
        dr = r0 + jnp.arange(kh) - r + NA_ROWS_MAX - 1
        bias = rpb[:, dr[:, None, None], dc[None, :, :]]
        bias = jnp.where(col_in[None, None], bias, -jnp.inf)
        bias = bias.transpose(0, 2, 1, 3).reshape(NA_HEADS, GRID_W, n_band)
        s_lat = jnp.einsum('bhqd,bhkd->bhqk', q_row, k_band).astype(jnp.float32) * scale + bias
        s_ctx = jnp.einsum('bhqd,bhkd->bhqk', q_row, k_ctx).astype(jnp.float32) * scale
        prob = jax.nn.softmax(jnp.concatenate([s_lat, s_ctx], axis=-1), axis=-1).astype(v.dtype)
        return (jnp.einsum('bhqk,bhkd->bhqd', prob[..., :n_band], v_band)
                + jnp.einsum('bhqk,bhkd->bhqd', prob[..., n_band:], v_ctx))

    o = lax.map(row_block, (jnp.arange(rows), jnp.moveaxis(qg, 2, 0)))
    return o.transpose(1, 0, 3, 2, 4).reshape(b, l, NA_WIDTH)


def ctx_attention(q_c, k_c, v_c):
    s = jnp.einsum('bhqd,bhkd->bhqk', q_c, k_c).astype(jnp.float32) * NA_HEAD_DIM ** -0.5
    prob = jax.nn.softmax(s, axis=-1).astype(v_c.dtype)
    o = jnp.einsum('bhqk,bhkd->bhqd', prob, v_c)
    b, _, n, _ = o.shape
    return o.transpose(0, 2, 1, 3).reshape(b, n, NA_WIDTH)


def peer_ffn(h, w_q, subkeys, u_tab, v_tab):
    b, l, d = h.shape
    q = (h @ w_q).reshape(b, l, PEER_HEADS, 2, PEER_HALF)
    s = jnp.einsum('blhnd,nkd->blhnk', q, subkeys).astype(jnp.float32)
    s_top, i_top = lax.top_k(s, PEER_TOPK)
    n_cand = PEER_TOPK * PEER_TOPK
    cand = (s_top[..., 0, :, None] + s_top[..., 1, None, :]).reshape(b, l, PEER_HEADS, n_cand)
    cand_idx = (i_top[..., 0, :, None] * PEER_KEYS + i_top[..., 1, None, :]).reshape(b, l, PEER_HEADS, n_cand)
    best, pos = lax.top_k(cand, PEER_TOPK)
    expert = jnp.take_along_axis(cand_idx, pos, axis=-1)
    gate = jax.nn.softmax(best, axis=-1)
    n_blk = (b * l) // PEER_BLOCK

    def block(args):
        h_blk, e_blk, g_blk = args
        a = jnp.einsum('td,thkd->thk', h_blk, u_tab[e_blk])
        w = (g_blk * jax.nn.gelu(a.astype(jnp.float32))).astype(v_tab.dtype)
        return jnp.einsum('thk,thkd->td', w, v_tab[e_blk])

    out = lax.map(block, (h.reshape(n_blk, PEER_BLOCK, d),
                          expert.reshape(n_blk, PEER_BLOCK, PEER_HEADS, PEER_TOPK),
                          gate.reshape(n_blk, PEER_BLOCK, PEER_HEADS, PEER_TOPK)))
    return out.reshape(b, l, d)


def hybrid_layer(x, ctx, c, c_ctx, p, with_ctx_out):
    shift1, scale1, gate1, shift2, scale2, gate2 = adaln(c[:, None, :], p['w_mod'], p['b_mod'], 6)
    mod_c = adaln(c_ctx[None, None, :], p['w_mod'], p['b_mod'], 6 if with_ctx_out else 2)

    hx = modulate(x, p['norm1_g'], shift1, scale1)
    hc = modulate(ctx, p['norm1_g'], mod_c[0], mod_c[1])
    px = hx @ p['w_in']
    u_x, k_x, v_x, q_x, ga_x, gb_x = jnp.split(px, [COL_K, COL_V, COL_Q, COL_GA, COL_GB], axis=-1)
    pc = hc @ p['w_in'][:, :(IN_COLS if with_ctx_out else CTX_COLS)]
    u_c, k_c, v_c = pc[..., :COL_K], pc[..., COL_K:COL_V], pc[..., COL_V:COL_Q]
    kc, vc = ctx_heads(k_c), ctx_heads(v_c)

    y_ssm_x, y_ssm_c = s5_mixer(u_x, u_c, p, with_ctx_out)
    branch_a = (s5_glu(y_ssm_x, p) @ p['w_branch_a']).astype(x.dtype)
    branch_b = na_mixer(q_x, k_x, v_x, kc, vc, p['na_rpb']) @ p['w_branch_b']
    merged = jax.nn.sigmoid(ga_x) * branch_a + jax.nn.sigmoid(gb_x) * branch_b
    x = x + gate1 * (merged @ p['w_out'])
    x = x + gate2 * peer_ffn(modulate(x, p['norm2_g'], shift2, scale2),
                             p['peer_w_q'], p['peer_subkeys'], p['peer_u'], p['peer_v'])

    if with_ctx_out:
        _, _, c_gate1, c_shift2, c_scale2, c_gate2 = mod_c
        q_c, ga_c, gb_c = pc[..., COL_Q:COL_GA], pc[..., COL_GA:COL_GB], pc[..., COL_GB:]
        branch_a_c = (s5_glu(y_ssm_c, p) @ p['w_branch_a']).astype(ctx.dtype)
        branch_b_c = ctx_attention(ctx_heads(q_c), kc, vc) @ p['w_branch_b']
        merged_c = jax.nn.sigmoid(ga_c) * branch_a_c + jax.nn.sigmoid(gb_c) * branch_b_c
        ctx = ctx + c_gate1 * (merged_c @ p['w_out'])
        ctx = ctx + c_gate2 * peer_ffn(modulate(ctx, p['norm2_g'], c_shift2, c_scale2),
                                       p['peer_w_q'], p['peer_subkeys'], p['peer_u'], p['peer_v'])
    return x, ctx


def setup_inputs(seed: int = 0) -> dict:
    key = jax.random.key(seed)
    ks = jax.random.split(key, 28)
    f32 = jnp.float32

    def nrm(k, shape, scale):
        return jax.random.normal(k, shape, f32) * scale

    L = DEPTH
    G, P, H = SSM_GROUPS, SSM_STATE, SSM_GROUP
    n_idx = jnp.arange(SSM_STATE, dtype=f32)
    return {
        'x': nrm(ks[0], (BATCH, SEQ, D_MODEL), 1.0),
        'c': nrm(ks[1], (BATCH, D_MODEL), 1.0),
        'ctx': nrm(ks[2], (BATCH, CTX_LEN, D_MODEL), 1.0),
        'c_ctx': nrm(ks[3], (D_MODEL,), 1.0),
        'w_mod': nrm(ks[4], (L, D_MODEL, 6 * D_MODEL), 0.5 * D_MODEL ** -0.5),
        'b_mod': nrm(ks[5], (L, 6 * D_MODEL), 0.01),
        'norm1_g': 1.0 + nrm(ks[6], (L, D_MODEL), 0.01),
        'norm2_g': 1.0 + nrm(ks[7], (L, D_MODEL), 0.01),
        'w_in': nrm(ks[8], (L, D_MODEL, IN_COLS), D_MODEL ** -0.5),
        'ssm_a_re': -0.5 + nrm(ks[9], (L, 2, G, P), 0.01),
        'ssm_a_im': math.pi * n_idx + nrm(ks[10], (L, 2, G, P), 0.01),
        'ssm_log_dt': jax.random.uniform(ks[11], (L, 2, G), f32, math.log(1e-3), math.log(1e-1)),
        'ssm_b_re': nrm(ks[12], (L, 2, G, P, H), (2 * H) ** -0.5),
        'ssm_b_im': nrm(ks[13], (L, 2, G, P, H), (2 * H) ** -0.5),
        'ssm_c_re': nrm(ks[14], (L, 2, G, H, P), P ** -0.5),
        'ssm_c_im': nrm(ks[15], (L, 2, G, H, P), P ** -0.5),
        'ssm_d': nrm(ks[16], (L, SSM_WIDTH), 1.0),
        'w_glu': nrm(ks[17], (L, SSM_WIDTH, SSM_WIDTH), SSM_WIDTH ** -0.5),
        'b_glu': nrm(ks[18], (L, SSM_WIDTH), 0.01),
        'w_branch_a': nrm(ks[19], (L, SSM_WIDTH, D_MODEL), SSM_WIDTH ** -0.5),
        'w_branch_b': nrm(ks[20], (L, NA_WIDTH, D_MODEL), NA_WIDTH ** -0.5),
        'na_rpb': nrm(ks[21], (L, NA_HEADS, 2 * NA_ROWS_MAX - 1, 2 * NA_COLS - 1), 0.02),
        'w_out': nrm(ks[22], (L, D_MODEL, D_MODEL), D_MODEL ** -0.5),
        'peer_w_q': nrm(ks[23], (L, D_MODEL, PEER_HEADS * PEER_QDIM), D_MODEL ** -0.5),
        'peer_subkeys': nrm(ks[24], (L, 2, PEER_KEYS, PEER_HALF), PEER_HALF ** -0.5),
        'peer_u': nrm(ks[25], (L, PEER_EXPERTS, D_MODEL), D_MODEL ** -0.5),
        'peer_v': nrm(ks[26], (L, PEER_EXPERTS, D_MODEL), PEER_HEADS ** -0.5),
        'final_g': 1.0 + nrm(ks[27], (D_MODEL,), 0.01),
    }


def reference(x, c, ctx, c_ctx, w_mod, b_mod, norm1_g, norm2_g, w_in,
              ssm_a_re, ssm_a_im, ssm_log_dt, ssm_b_re, ssm_b_im, ssm_c_re, ssm_c_im, ssm_d,
              w_glu, b_glu, w_branch_a, w_branch_b, na_rpb, w_out,
              peer_w_q, peer_subkeys, peer_u, peer_v, final_g):
    for i in range(DEPTH):
        p = {
            'w_mod': w_mod[i], 'b_mod': b_mod[i], 'norm1_g': norm1_g[i], 'norm2_g': norm2_g[i],
            'w_in': w_in[i],
            'ssm_a_re': ssm_a_re[i], 'ssm_a_im': ssm_a_im[i], 'ssm_log_dt': ssm_log_dt[i],
            'ssm_b_re': ssm_b_re[i], 'ssm_b_im': ssm_b_im[i], 'ssm_c_re': ssm_c_re[i], 'ssm_c_im': ssm_c_im[i],
            'ssm_d': ssm_d[i], 'w_glu': w_glu[i], 'b_glu': b_glu[i],
            'w_branch_a': w_branch_a[i], 'w_branch_b': w_branch_b[i], 'na_rpb': na_rpb[i], 'w_out': w_out[i],
            'peer_w_q': peer_w_q[i], 'peer_subkeys': peer_subkeys[i], 'peer_u': peer_u[i], 'peer_v': peer_v[i],
        }
        x, ctx = hybrid_layer(x, ctx, c, c_ctx, p, with_ctx_out=(i < DEPTH - 1))
    return rms_norm(x, final_g)
```

```python
import functools
import math

import jax
import jax.numpy as jnp
from jax import lax
from jax.experimental import pallas as pl
from jax.experimental.pallas import tpu as pltpu

EPS = 1e-6
F32 = jnp.float32
BF16 = jnp.bfloat16

SSM_GROUP = 16
SSM_CHUNK = 16
NA_HEAD_DIM = 64
NA_ROWS_MAX = 8
NA_COLS = 16
GRID_W = 64
PEER_TOPK = 16
LANES = 128
NEG = -1e30

VMEM_LIMIT = 56 * 1024 * 1024


def _cparams(sem):
    return pltpu.CompilerParams(dimension_semantics=sem, vmem_limit_bytes=VMEM_LIMIT)


def _sigmoid(x):
    return 1.0 / (1.0 + jnp.exp(-x))


def _gelu(x):
    return 0.5 * x * (1.0 + jnp.tanh(math.sqrt(2.0 / math.pi) * (x + 0.044715 * (x * x * x))))


def _rms(x, g):
    return x * lax.rsqrt(jnp.mean(x * x, axis=-1, keepdims=True) + EPS) * g


def _adaln_kernel(c_ref, w_ref, b_ref, o_ref):
    c = c_ref[...]
    a = c * _sigmoid(c)
    o_ref[...] = jnp.dot(a, w_ref[...], preferred_element_type=F32,
                         precision=lax.Precision.HIGHEST) + b_ref[...]


def _adaln(cond, w_mod, b_mod):
    rows, d = cond.shape
    cols = w_mod.shape[1]
    tn = 1536
    return pl.pallas_call(
        _adaln_kernel,
        grid=(cols // tn,),
        in_specs=[pl.BlockSpec((rows, d), lambda j: (0, 0)),
                  pl.BlockSpec((d, tn), lambda j: (0, j)),
                  pl.BlockSpec((1, tn), lambda j: (0, j))],
        out_specs=pl.BlockSpec((rows, tn), lambda j: (0, j)),
        out_shape=jax.ShapeDtypeStruct((rows, cols), F32),
        compiler_params=_cparams(("arbitrary",)),
        name="adaln",
    )(cond, w_mod, b_mod.reshape(1, cols))


def _inproj_kernel(x_ref, g_ref, shift_ref, scale_ref, w_ref, o_ref):
    h = _rms(x_ref[0], g_ref[...]) * (1.0 + scale_ref[0]) + shift_ref[0]
    o_ref[0] = jnp.dot(h.astype(BF16), w_ref[...], preferred_element_type=F32).astype(o_ref.dtype)


def _inproj(x, g, shift, scale, w, tm):
    b, n, d = x.shape
    cols = w.shape[1]
    per_batch = shift.shape[0] == b
    mod_map = (lambda i, j: (i, 0, 0)) if per_batch else (lambda i, j: (0, 0, 0))
    return pl.pallas_call(
        _inproj_kernel,
        grid=(b, n // tm),
        in_specs=[pl.BlockSpec((1, tm, d), lambda i, j: (i, j, 0)),
                  pl.BlockSpec((1, d), lambda i, j: (0, 0)),
                  pl.BlockSpec((1, 1, d), mod_map),
                  pl.BlockSpec((1, 1, d), mod_map),
                  pl.BlockSpec((d, cols), lambda i, j: (0, 0))],
        out_specs=pl.BlockSpec((1, tm, cols), lambda i, j: (i, j, 0)),
        out_shape=jax.ShapeDtypeStruct((b, n, cols), BF16),
        compiler_params=_cparams(("parallel", "parallel")),
        name="inproj",
    )(x, g, shift, scale, w)


def _s5_tables(a_re, a_im, log_dt, b_re, b_im, c_re, c_im, d_skip):
    L, H = SSM_CHUNK, SSM_GROUP
    hp = lax.Precision.HIGHEST
    lam = lax.complex(a_re.astype(F32), a_im.astype(F32))
    dt = jnp.exp(log_dt.astype(F32))[..., None]
    lam_dt = lam * dt
    lam_bar = jnp.exp(lam_dt)
    b_bar = ((lam_bar - 1) / lam)[..., None] * lax.complex(b_re.astype(F32), b_im.astype(F32))
    cm = lax.complex(c_re.astype(F32), c_im.astype(F32))
    k = jnp.arange(L + 1, dtype=F32)
    pw = jnp.exp(lam_dt[None] * k[:, None, None, None].astype(lam_dt.dtype))
    g = lam.shape[1]
    p = lam.shape[2]

    kern = jnp.einsum('dghp,tdgp,dgpk->tdghk', cm, pw[:L], b_bar, precision=hp).real
    t_idx = jnp.arange(L)
    eye = jnp.eye(H, dtype=F32)

    def intra(d, lag):
        valid = lag >= 0
        kk = kern[jnp.clip(lag, 0, L - 1), d]
        kk = jnp.where(valid[:, :, None, None, None], kk, 0.0)
        return kk.transpose(2, 0, 4, 1, 3)

    lag_f = t_idx[None, :] - t_idx[:, None]
    lag_r = t_idx[:, None] - t_idx[None, :]
    m_f = intra(0, lag_f)
    skip = d_skip.astype(F32).reshape(g, H)
    m_f = m_f + (jnp.eye(L, dtype=F32)[None, :, None, :, None] * eye[None, None, :, None, :]
                 * skip[:, None, None, None, :])
    m_r = intra(1, lag_r)
    m_f = m_f.reshape(g, L * H, L * H)
    m_r = m_r.reshape(g, L * H, L * H)

    def m_in(d, expo):
        z = pw[expo, d][:, :, :, None] * b_bar[d][None]
        z = z.transpose(1, 0, 3, 2).reshape(g, L * H, p)
        return (jnp.concatenate([z.real, z.imag], -1), jnp.concatenate([z.imag, z.real], -1))

    in_f, insw_f = m_in(0, L - 1 - t_idx)
    in_r, insw_r = m_in(1, t_idx)

    def m_out(d, expo):
        w = cm[d][None] * pw[expo, d][:, :, None, :]
        w = w.transpose(1, 3, 0, 2).reshape(g, p, L * H)
        return jnp.concatenate([w.real, -w.imag], 1)

    out_f = m_out(0, t_idx + 1)
    out_r = m_out(1, L - t_idx)

    def a_rows(d):
        al = pw[L, d]
        return [jnp.concatenate([al.real, al.real], -1),
                jnp.concatenate([-al.imag, al.imag], -1),
                jnp.concatenate([al.imag, -al.imag], -1)]

    rows = a_rows(0) + a_rows(1)
    rows = rows + [jnp.zeros_like(rows[0])] * 2
    a_tab = jnp.stack(rows, 1)
    w_all = jnp.concatenate([m_f, m_r, in_f, insw_f, in_r, insw_r], -1).astype(BF16)
    return w_all, jnp.stack([out_f, out_r], 1).astype(BF16), a_tab


def _s5_kernel(u_ref, w_ref, mo_ref, a_ref, y_ref, z_ref, sf_ref, sr_ref, *, nb, n_ctx, n_lat):
    lh = SSM_CHUNK * SSM_GROUP
    u = u_ref[0]
    z_ref[...] = jnp.dot(u, w_ref[0, :, 2 * lh:], preferred_element_type=F32)
    a = a_ref[0]
    n_all = n_ctx + n_lat

    def run(col, a1, a2, a2sw, order, s_ref, steps, carry):
        def body(kk, c):
            s, ssw = c
            r0 = pl.multiple_of(order(kk) * nb, nb)
            s_ref[pl.ds(r0, nb), :] = s
            x = z_ref[pl.ds(r0, nb), col:col + LANES]
            xsw = z_ref[pl.ds(r0, nb), col + LANES:col + 2 * LANES]
            return a1 * s + a2 * ssw + x, a1 * ssw + a2sw * s + xsw
        return lax.fori_loop(0, steps, body, carry, unroll=8)

    zero = jnp.zeros((nb, LANES), F32)
    run(0, a[0:1], a[1:2], a[2:3], lambda kk: kk, sf_ref, n_all, (zero, zero))
    c = run(2 * LANES, a[3:4], a[4:5], a[5:6], lambda kk: n_ctx - 1 - kk, sr_ref, n_ctx, (zero, zero))
    run(2 * LANES, a[3:4], a[4:5], a[5:6], lambda kk: n_all - 1 - kk, sr_ref, n_lat, c)

    lat0 = n_ctx * nb
    ul = u_ref[0, lat0:, :]
    yi = jnp.dot(ul, w_ref[0, :, :2 * lh], preferred_element_type=F32)
    y = yi[:, :lh] + yi[:, lh:]
    y = y + jnp.dot(sf_ref[lat0:, :].astype(BF16), mo_ref[0, 0], preferred_element_type=F32)
    y = y + jnp.dot(sr_ref[lat0:, :].astype(BF16), mo_ref[0, 1], preferred_element_type=F32)
    y_ref[0] = y


def _s5(u_all, w_all, m_out, a_tab, nb, n_ctx, n_lat):
    g, r, lh = u_all.shape
    rl = n_lat * nb
    return pl.pallas_call(
        functools.partial(_s5_kernel, nb=nb, n_ctx=n_ctx, n_lat=n_lat),
        grid=(g,),
        in_specs=[pl.BlockSpec((1, r, lh), lambda i: (i, 0, 0)),
                  pl.BlockSpec((1, lh, 4 * lh), lambda i: (i, 0, 0)),
                  pl.BlockSpec((1, 2, LANES, lh), lambda i: (i, 0, 0, 0)),
                  pl.BlockSpec((1, 8, LANES), lambda i: (i, 0, 0))],
        out_specs=pl.BlockSpec((1, rl, lh), lambda i: (i, 0, 0)),
        out_shape=jax.ShapeDtypeStruct((g, rl, lh), F32),
        scratch_shapes=[pltpu.VMEM((r, 4 * LANES), F32),
                        pltpu.VMEM((r, LANES), F32),
                        pltpu.VMEM((r, LANES), F32)],
        compiler_params=_cparams(("parallel",)),
        name="s5_scan",
    )(u_all, w_all, m_out, a_tab)


def _to_groups(u):
    b, n, w = u.shape
    g = w // SSM_GROUP
    return (u.reshape(b, n // SSM_CHUNK, SSM_CHUNK, g, SSM_GROUP)
            .transpose(3, 1, 0, 2, 4).reshape(g, n // SSM_CHUNK, b, SSM_CHUNK * SSM_GROUP))


def _na_bias(rpb):
    kh = NA_ROWS_MAX
    cols = jnp.arange(GRID_W)
    c0 = jnp.clip(cols - NA_COLS // 2, 0, GRID_W - NA_COLS)
    col_in = (cols[None, :] >= c0[:, None]) & (cols[None, :] < c0[:, None] + NA_COLS)
    dc = jnp.clip(cols[None, :] - cols[:, None] + NA_COLS - 1, 0, 2 * NA_COLS - 2)
    var = jnp.arange(kh)
    dr = jnp.arange(kh)[None, :] - var[:, None] + NA_ROWS_MAX - 1
    bias = rpb.astype(F32)[:, dr[:, :, None, None], dc[None, None, :, :]]
    bias = jnp.where(col_in[None, None, None], bias, NEG)
    heads = rpb.shape[0]
    return bias.transpose(1, 0, 3, 2, 4).reshape(kh, heads, GRID_W, kh * GRID_W)


def _na_kernel(q_ref, k_ref, v_ref, kc_ref, vc_ref, bias_ref, o_ref, *, rows, kh):
    r = pl.program_id(1)
    r0 = jnp.clip(r - kh // 2, 0, rows - kh)
    start = pl.multiple_of(r0 * GRID_W, GRID_W)
    nband = kh * GRID_W
    lane = lax.broadcasted_iota(jnp.int32, (1, LANES), 1)
    lo = lane < NA_HEAD_DIM
    scale = NA_HEAD_DIM ** -0.5
    nt = (((1,), (1,)), ((), ()))
    n_pairs = q_ref.shape[2] // LANES
    for hp in range(n_pairs):
        cs = slice(hp * LANES, (hp + 1) * LANES)
        q2 = q_ref[0, :, cs]
        k2 = k_ref[0, pl.ds(start, nband), cs]
        v2 = v_ref[0, pl.ds(start, nband), cs]
        kc2 = kc_ref[0, :, cs]
        vc2 = vc_ref[0, :, cs]
        outs = []
        for half in range(2):
            qm = jnp.where(lo if half == 0 else jnp.logical_not(lo), q2, jnp.zeros_like(q2))
            s = lax.dot_general(qm, k2, nt, preferred_element_type=F32) * scale + bias_ref[0, 2 * hp + half]
            sc = lax.dot_general(qm, kc2, nt, preferred_element_type=F32) * scale
            m = jnp.maximum(jnp.max(s, axis=-1, keepdims=True), jnp.max(sc, axis=-1, keepdims=True))
            p = jnp.exp(s - m)
            pc = jnp.exp(sc - m)
            den = jnp.sum(p, axis=-1, keepdims=True) + jnp.sum(pc, axis=-1, keepdims=True)
            o = (jnp.dot(p.astype(BF16), v2, preferred_element_type=F32)
                 + jnp.dot(pc.astype(BF16), vc2, preferred_element_type=F32))
            outs.append(o / den)
        o_ref[0, :, cs] = jnp.where(lo, outs[0], outs[1]).astype(o_ref.dtype)


def _na(px, pc, bias, width, col_k, col_v, col_q):
    b, n, _ = px.shape
    n_ctx = pc.shape[1]
    rows = n // GRID_W
    kh = min(NA_ROWS_MAX, rows)
    assert kh == NA_ROWS_MAX, "sequence must span at least NA_ROWS_MAX grid rows"
    heads = bias.shape[1]

    def bias_map(i, r):
        return (r - jnp.clip(r - kh // 2, 0, rows - kh), 0, 0, 0)

    return pl.pallas_call(
        functools.partial(_na_kernel, rows=rows, kh=kh),
        grid=(b, rows),
        in_specs=[pl.BlockSpec((1, GRID_W, width), lambda i, r: (i, r, col_q // width)),
                  pl.BlockSpec((1, n, width), lambda i, r: (i, 0, col_k // width)),
                  pl.BlockSpec((1, n, width), lambda i, r: (i, 0, col_v // width)),
                  pl.BlockSpec((1, n_ctx, width), lambda i, r: (i, 0, col_k // width)),
                  pl.BlockSpec((1, n_ctx, width), lambda i, r: (i, 0, col_v // width)),
                  pl.BlockSpec((1, heads, GRID_W, kh * GRID_W), bias_map)],
        out_specs=pl.BlockSpec((1, GRID_W, width), lambda i, r: (i, r, 0)),
        out_shape=jax.ShapeDtypeStruct((b, n, width), BF16),
        compiler_params=_cparams(("parallel", "arbitrary")),
        name="na_mixer",
    )(px, px, px, pc, pc, bias)


def _merge_kernel(x_ref, y_ref, o_ref, ga_ref, gb_ref, gate_ref, shift_ref, scale_ref, g2_ref,
                  wglu_ref, bglu_ref, wa_ref, wb_ref, wo_ref, x1_ref, h2_ref):
    z = _gelu(y_ref[0])
    gl = jnp.dot(z.astype(BF16), wglu_ref[...], preferred_element_type=F32) + bglu_ref[...]
    zz = z * _sigmoid(gl)
    ba = jnp.dot(zz.astype(BF16), wa_ref[...], preferred_element_type=F32)
    bb = jnp.dot(o_ref[0], wb_ref[...], preferred_element_type=F32)
    merged = _sigmoid(ga_ref[0].astype(F32)) * ba + _sigmoid(gb_ref[0].astype(F32)) * bb
    x1 = x_ref[0] + gate_ref[0] * jnp.dot(merged.astype(BF16), wo_ref[...], preferred_element_type=F32)
    x1_ref[0] = x1
    h2_ref[0] = (_rms(x1, g2_ref[...]) * (1.0 + scale_ref[0]) + shift_ref[0]).astype(h2_ref.dtype)


def _merge(x, y_ssm, o_na, px, gate1, shift2, scale2, g2, w_glu, b_glu, w_a, w_b, w_o, col_ga, col_gb, tm):
    b, n, d = x.shape
    w = y_ssm.shape[2]
    tok = lambda i, j: (i, j, 0)
    per_b = lambda i, j: (i, 0, 0)
    full = lambda i, j: (0, 0)
    return pl.pallas_call(
        _merge_kernel,
        grid=(b, n // tm),
        in_specs=[pl.BlockSpec((1, tm, d), tok),
                  pl.BlockSpec((1, tm, w), tok),
                  pl.BlockSpec((1, tm, w), tok),
                  pl.BlockSpec((1, tm, d), lambda i, j: (i, j, col_ga // d)),
                  pl.BlockSpec((1, tm, d), lambda i, j: (i, j, col_gb // d)),
                  pl.BlockSpec((1, 1, d), per_b),
                  pl.BlockSpec((1, 1, d), per_b),
                  pl.BlockSpec((1, 1, d), per_b),
                  pl.BlockSpec((1, d), full),
                  pl.BlockSpec((w, w), full),
                  pl.BlockSpec((1, w), full),
                  pl.BlockSpec((w, d), full),
                  pl.BlockSpec((w, d), full),
                  pl.BlockSpec((d, d), full)],
        out_specs=[pl.BlockSpec((1, tm, d), tok), pl.BlockSpec((1, tm, d), tok)],
        out_shape=[jax.ShapeDtypeStruct((b, n, d), F32), jax.ShapeDtypeStruct((b, n, d), BF16)],
        compiler_params=_cparams(("parallel", "parallel")),
        name="merge",
    )(x, y_ssm, o_na, px, px, gate1, shift2, scale2, g2, w_glu, b_glu, w_a, w_b, w_o)


def _top16(s):
    iota = lax.broadcasted_iota(jnp.int32, s.shape, 0)
    n = s.shape[0]
    cur = s
    rank = jnp.full(s.shape, 127.0, F32)
    vals = []
    for k in range(PEER_TOPK):
        m = jnp.max(cur, axis=0, keepdims=True)
        idx = jnp.min(jnp.where(cur == m, iota, n), axis=0, keepdims=True)
        hit = iota == idx
        rank = jnp.where(hit, float(k), rank)
        cur = jnp.where(hit, -jnp.inf, cur)
        vals.append(m)
    return rank, jnp.concatenate(vals, axis=0)


def _route_kernel(h_ref, wq_ref, sk_ref, re_ref, nc_ref, *, heads):
    nt = (((1,), (1,)), ((), ()))
    h = h_ref[...]
    tm = h.shape[0]
    kq = sk_ref.shape[2]
    for hd in range(heads):
        ranks, tops, raws = [], [], []
        for half in range(2):
            row = (hd * 2 + half) * kq
            qt = lax.dot_general(wq_ref[row:row + kq, :], h, nt, preferred_element_type=F32)
            st = jnp.dot(sk_ref[half], qt.astype(BF16), preferred_element_type=F32)
            rk, tv = _top16(st)
            ranks.append(rk)
            tops.append(tv)
            raws.append(st)
        v0, v1 = tops
        cand = jnp.concatenate([v0[a:a + 1] + v1 for a in range(PEER_TOPK)], axis=0)
        iota = lax.broadcasted_iota(jnp.int32, cand.shape, 0)
        cur = cand
        sel = jnp.zeros(cand.shape, F32)
        for _ in range(PEER_TOPK):
            m = jnp.max(cur, axis=0, keepdims=True)
            idx = jnp.min(jnp.where(cur == m, iota, cand.shape[0]), axis=0, keepdims=True)
            hit = iota == idx
            sel = jnp.where(hit, 1.0, sel)
            cur = jnp.where(hit, -jnp.inf, cur)
        e0 = jnp.exp(v0 - v0[0:1])
        e1 = jnp.exp(v1 - v1[0:1])
        prod = jnp.concatenate([e0[a:a + 1] * e1 for a in range(PEER_TOPK)], axis=0)
        zsum = jnp.sum(sel * prod, axis=0, keepdims=True)
        cnt = jnp.zeros(raws[0].shape, F32)
        for a in range(PEER_TOPK):
            n_a = jnp.sum(sel[a * PEER_TOPK:(a + 1) * PEER_TOPK], axis=0, keepdims=True)
            cnt = jnp.where(ranks[0] == float(a), n_a, cnt)
        coef = jnp.where(ranks[0] < float(PEER_TOPK), jnp.exp(raws[0] - v0[0:1]) / zsum, 0.0)
        e1f = jnp.where(ranks[1] < float(PEER_TOPK), jnp.exp(raws[1] - v1[0:1]), 0.0)
        re_ref[0, hd] = ranks[1].astype(re_ref.dtype)
        re_ref[1, hd] = e1f.astype(re_ref.dtype)
        nc_ref[0, hd] = cnt
        nc_ref[1, hd] = coef


def _route(h2, wq_t, subkeys, heads, tm):
    t, d = h2.shape
    keys = subkeys.shape[1]
    return pl.pallas_call(
        functools.partial(_route_kernel, heads=heads),
        grid=(t // tm,),
        in_specs=[pl.BlockSpec((tm, d), lambda i: (i, 0)),
                  pl.BlockSpec(wq_t.shape, lambda i: (0, 0)),
                  pl.BlockSpec(subkeys.shape, lambda i: (0, 0, 0))],
        out_specs=[pl.BlockSpec((2, heads, keys, tm), lambda i: (0, 0, 0, i)),
                   pl.BlockSpec((2, heads, keys, tm), lambda i: (0, 0, 0, i))],
        out_shape=[jax.ShapeDtypeStruct((2, heads, keys, t), BF16),
                   jax.ShapeDtypeStruct((2, heads, keys, t), F32)],
        compiler_params=_cparams(("parallel",)),
        name="peer_route",
    )(h2, wq_t, subkeys)


def _peer_kernel(h_ref, u_ref, vt_ref, re_ref, nc_ref, x1_ref, gate_ref, fg_ref, o_ref,
                 acc_ref, a_ref, w_ref, *, heads):
    j = pl.program_id(1)
    nt = (((1,), (1,)), ((), ()))

    @pl.when(j == 0)
    def _():
        acc_ref[...] = jnp.zeros_like(acc_ref)

    a_ref[...] = lax.dot_general(u_ref[...], h_ref[...], nt, preferred_element_type=F32)
    te = u_ref.shape[0]
    for sl in range(te // LANES):
        rs = slice(sl * LANES, (sl + 1) * LANES)
        g = None
        for hd in range(heads):
            nrow = nc_ref[0, hd, sl:sl + 1, :].astype(BF16)
            crow = nc_ref[1, hd, sl:sl + 1, :].astype(BF16)
            term = jnp.where(re_ref[0, hd] < nrow, re_ref[1, hd], jnp.zeros_like(re_ref[1, hd])) * crow
            g = term if g is None else g + term
        w_ref[rs, :] = (_gelu(a_ref[rs, :]) * g.astype(F32)).astype(BF16)
    acc_ref[...] += jnp.dot(vt_ref[...], w_ref[...], preferred_element_type=F32)

    @pl.when(j == pl.num_programs(1) - 1)
    def _():
        x2 = x1_ref[...] + gate_ref[0] * acc_ref[...].T
        o_ref[...] = _rms(x2, fg_ref[...])


def _peer(h2, u_tab, v_t, route_re, route_nc, x1, gate2, final_g, n_per_batch, tm, te):
    t, d = h2.shape
    e = u_tab.shape[0]
    heads, keys = route_re.shape[1], route_re.shape[2]
    assert keys == LANES and te % (8 * LANES) == 0 and n_per_batch % tm == 0
    return pl.pallas_call(
        functools.partial(_peer_kernel, heads=heads),
        grid=(t // tm, e // te),
        in_specs=[pl.BlockSpec((tm, d), lambda i, j: (i, 0)),
                  pl.BlockSpec((te, d), lambda i, j: (j, 0)),
                  pl.BlockSpec((d, te), lambda i, j: (0, j)),
                  pl.BlockSpec((2, heads, keys, tm), lambda i, j: (0, 0, 0, i)),
                  pl.BlockSpec((2, heads, te // LANES, tm), lambda i, j: (0, 0, j, i)),
                  pl.BlockSpec((tm, d), lambda i, j: (i, 0)),
                  pl.BlockSpec((1, 1, d), lambda i, j: ((i * tm) // n_per_batch, 0, 0)),
                  pl.BlockSpec((1, d), lambda i, j: (0, 0))],
        out_specs=pl.BlockSpec((tm, d), lambda i, j: (i, 0)),
        out_shape=jax.ShapeDtypeStruct((t, d), F32),
        scratch_shapes=[pltpu.VMEM((d, tm), F32),
                        pltpu.VMEM((te, tm), F32),
                        pltpu.VMEM((te, tm), BF16)],
        compiler_params=_cparams(("parallel", "arbitrary")),
        name="peer_dense",
    )(h2, u_tab, v_t, route_re, route_nc, x1, gate2, final_g)


def _layer(x, ctx, c, c_ctx, p, final_g):
    b, n, d = x.shape
    n_ctx = ctx.shape[1]
    width = p['w_glu'].shape[0]
    col_k, col_v, col_q = width, 2 * width, 3 * width
    col_ga, col_gb = 4 * width, 4 * width + d
    heads_p = p['peer_w_q'].shape[1] // (2 * p['peer_subkeys'].shape[2])
    assert b % 8 == 0 and n % (GRID_W * SSM_CHUNK) == 0 and n_ctx % SSM_CHUNK == 0
    assert width % LANES == 0 and col_ga % d == 0

    rows = 16
    cond = jnp.zeros((rows, d), F32).at[:b].set(c).at[b].set(c_ctx)
    mod = _adaln(cond, p['w_mod'], p['b_mod'])
    chunks = [mod[:, i * d:(i + 1) * d] for i in range(6)]
    shift1, scale1, gate1, shift2, scale2, gate2 = [m[:b, None, :] for m in chunks]
    shift_c, scale_c = chunks[0][b:b + 1, None, :], chunks[1][b:b + 1, None, :]

    w_in = p['w_in'].astype(BF16)
    g1 = p['norm1_g'].reshape(1, d)
    px = _inproj(x, g1, shift1, scale1, w_in, tm=256)
    pc = _inproj(ctx, g1, shift_c, scale_c, w_in[:, :col_q], tm=n_ctx)

    w_all, m_out, a_tab = _s5_tables(p['ssm_a_re'], p['ssm_a_im'], p['ssm_log_dt'], p['ssm_b_re'],
                                     p['ssm_b_im'], p['ssm_c_re'], p['ssm_c_im'], p['ssm_d'])
    cc, cl = n_ctx // SSM_CHUNK, n // SSM_CHUNK
    u_all = jnp.concatenate([_to_groups(pc[..., :width]), _to_groups(px[..., :width])], axis=1)
    groups = u_all.shape[0]
    u_all = u_all.reshape(groups, (cc + cl) * b, SSM_CHUNK * SSM_GROUP)
    y_g = _s5(u_all, w_all, m_out, a_tab, b, cc, cl)
    y_ssm = (y_g.reshape(groups, cl, b, SSM_CHUNK, SSM_GROUP)
             .transpose(2, 1, 3, 0, 4).reshape(b, n, width))

    o_na = _na(px, pc, _na_bias(p['na_rpb']), width, col_k, col_v, col_q)

    x1, h2 = _merge(x, y_ssm, o_na, px, gate1, shift2, scale2, p['norm2_g'].reshape(1, d),
                    p['w_glu'].astype(BF16), p['b_glu'].reshape(1, width),
                    p['w_branch_a'].astype(BF16), p['w_branch_b'].astype(BF16),
                    p['w_out'].astype(BF16), col_ga, col_gb, tm=256)

    t = b * n
    h2f = h2.reshape(t, d)
    route_re, route_nc = _route(h2f, p['peer_w_q'].T.astype(BF16), p['peer_subkeys'].astype(BF16),
                                heads_p, tm=256)
    out = _peer(h2f, p['peer_u'].astype(BF16), p['peer_v'].T.astype(BF16), route_re, route_nc,
                x1.reshape(t, d), gate2, final_g.reshape(1, d), n, tm=512, te=1024)
    return out.reshape(b, n, d)


def kernel(x, c, ctx, c_ctx, w_mod, b_mod, norm1_g, norm2_g, w_in, ssm_a_re, ssm_a_im, ssm_log_dt,
           ssm_b_re, ssm_b_im, ssm_c_re, ssm_c_im, ssm_d, w_glu, b_glu, w_branch_a, w_branch_b, na_rpb,
           w_out, peer_w_q, peer_subkeys, peer_u, peer_v, final_g):
    assert w_mod.shape[0] == 1, "single-layer stack"
    p = {
        'w_mod': w_mod[0], 'b_mod': b_mod[0], 'norm1_g': norm1_g[0], 'norm2_g': norm2_g[0],
        'w_in': w_in[0],
        'ssm_a_re': ssm_a_re[0], 'ssm_a_im': ssm_a_im[0], 'ssm_log_dt': ssm_log_dt[0],
        'ssm_b_re': ssm_b_re[0], 'ssm_b_im': ssm_b_im[0], 'ssm_c_re': ssm_c_re[0], 'ssm_c_im': ssm_c_im[0],
        'ssm_d': ssm_d[0], 'w_glu': w_glu[0], 'b_glu': b_glu[0],
        'w_branch_a': w_branch_a[0], 'w_branch_b': w_branch_b[0], 'na_rpb': na_rpb[0], 'w_out': w_out[0],
        'peer_w_q': peer_w_q[0], 'peer_subkeys': peer_subkeys[0], 'peer_u': peer_u[0], 'peer_v': peer_v[0],
    }
    return _layer(x, ctx, c, c_ctx, p, final_g)
```

```python
import functools
import math

import jax
import jax.numpy as jnp
import numpy as np
from jax import lax
from jax.experimental import pallas as pl
from jax.experimental.pallas import tpu as pltpu

EPS = 1e-6
F32 = jnp.float32
BF16 = jnp.bfloat16

SSM_GROUP = 16
SSM_CHUNK = 16
NA_HEAD_DIM = 64
NA_ROWS_MAX = 8
NA_COLS = 16
GRID_W = 64
PEER_TOPK = 16
LANES = 128
SUBLANES = 8
NEG = -1e30
MARK = 2.0 ** 100
PEER_COLS = 256

VMEM_LIMIT = 56 * 1024 * 1024


def _cparams(sem):
    return pltpu.CompilerParams(dimension_semantics=sem, vmem_limit_bytes=VMEM_LIMIT)


def _sigmoid(x):
    return 1.0 / (1.0 + jnp.exp(-x))


_GELU_K1 = -2.0 * math.sqrt(2.0 / math.pi) * math.log2(math.e)
_GELU_K2 = _GELU_K1 * 0.044715


def _gelu(x):
    t = x * (_GELU_K1 + _GELU_K2 * (x * x))
    return x / (1.0 + jnp.exp2(t))


def _rms(x, g):
    return x * lax.rsqrt(jnp.mean(x * x, axis=-1, keepdims=True) + EPS) * g


def _adaln_kernel(c_ref, w_ref, b_ref, o_ref):
    c = c_ref[...]
    a = c * _sigmoid(c)
    o_ref[...] = jnp.dot(a, w_ref[...], preferred_element_type=F32,
                         precision=lax.Precision.HIGHEST) + b_ref[...]


def _adaln(cond, w_mod, b_mod):
    rows, d = cond.shape
    cols = w_mod.shape[1]
    tn = 1536
    return pl.pallas_call(
        _adaln_kernel,
        grid=(cols // tn,),
        in_specs=[pl.BlockSpec((rows, d), lambda j: (0, 0)),
                  pl.BlockSpec((d, tn), lambda j: (0, j)),
                  pl.BlockSpec((1, tn), lambda j: (0, j))],
        out_specs=pl.BlockSpec((rows, tn), lambda j: (0, j)),
        out_shape=jax.ShapeDtypeStruct((rows, cols), F32),
        compiler_params=_cparams(("arbitrary",)),
        name="adaln",
    )(cond, w_mod, b_mod.reshape(1, cols))


def _inproj_kernel(x_ref, g_ref, shift_ref, scale_ref, w_ref, o_ref):
    h = _rms(x_ref[0], g_ref[...]) * (1.0 + scale_ref[0]) + shift_ref[0]
    o_ref[0] = jnp.dot(h.astype(BF16), w_ref[...], preferred_element_type=F32).astype(o_ref.dtype)


def _inproj(x, g, shift, scale, w, tm):
    b, n, d = x.shape
    cols = w.shape[1]
    per_batch = shift.shape[0] == b
    mod_map = (lambda i, j: (i, 0, 0)) if per_batch else (lambda i, j: (0, 0, 0))
    return pl.pallas_call(
        _inproj_kernel,
        grid=(b, n // tm),
        in_specs=[pl.BlockSpec((1, tm, d), lambda i, j: (i, j, 0)),
                  pl.BlockSpec((1, d), lambda i, j: (0, 0)),
                  pl.BlockSpec((1, 1, d), mod_map),
                  pl.BlockSpec((1, 1, d), mod_map),
                  pl.BlockSpec((d, cols), lambda i, j: (0, 0))],
        out_specs=pl.BlockSpec((1, tm, cols), lambda i, j: (i, j, 0)),
        out_shape=jax.ShapeDtypeStruct((b, n, cols), BF16),
        compiler_params=_cparams(("parallel", "parallel")),
        name="inproj",
    )(x, g, shift, scale, w)


def _cmul(ar, ai, br, bi):
    return ar * br - ai * bi, ar * bi + ai * br


def _s5_tables(a_re, a_im, log_dt, b_re, b_im, c_re, c_im, d_skip):
    L, H = SSM_CHUNK, SSM_GROUP
    hp = lax.Precision.HIGHEST
    ar, ai = a_re.astype(F32), a_im.astype(F32)
    dt = jnp.exp(log_dt.astype(F32))[..., None]
    xr, xi = ar * dt, ai * dt
    k = jnp.arange(L + 1, dtype=F32)[:, None, None, None]
    mag = jnp.exp(k * xr[None])
    pwr, pwi = mag * jnp.cos(k * xi[None]), mag * jnp.sin(k * xi[None])
    den = ar * ar + ai * ai
    ur, ui = pwr[1] - 1.0, pwi[1]
    qr, qi = (ur * ar + ui * ai) / den, (ui * ar - ur * ai) / den
    bbr, bbi = _cmul(qr[..., None], qi[..., None], b_re.astype(F32), b_im.astype(F32))
    cr, ci = c_re.astype(F32), c_im.astype(F32)
    g, p = ar.shape[1], ar.shape[2]

    wr, wi = _cmul(cr[None], ci[None], pwr[:L, :, :, None, :], pwi[:L, :, :, None, :])
    kern = (jnp.einsum('tdghp,dgpk->tdghk', wr, bbr, precision=hp)
            - jnp.einsum('tdghp,dgpk->tdghk', wi, bbi, precision=hp))
    t_idx = np.arange(L)
    lag_f = t_idx[None, :] - t_idx[:, None]
    sel_f = (lag_f[None] == t_idx[:, None, None]).astype(np.float32)
    sel_r = (-lag_f[None] == t_idx[:, None, None]).astype(np.float32)
    m_f = jnp.einsum('sjt,sghk->gjkth', sel_f, kern[:, 0], precision=hp)
    m_r = jnp.einsum('sjt,sghk->gjkth', sel_r, kern[:, 1], precision=hp)
    skip = d_skip.astype(F32).reshape(g, H)
    eye_l, eye_h = np.eye(L, dtype=np.float32), np.eye(H, dtype=np.float32)
    m_f = m_f + eye_l[None, :, None, :, None] * eye_h[None, None, :, None, :] * skip[:, None, None, None, :]
    m_f = m_f.reshape(g, L * H, L * H)
    m_r = m_r.reshape(g, L * H, L * H)

    def m_in(d, er, ei):
        zr, zi = _cmul(er[..., None], ei[..., None], bbr[d][None], bbi[d][None])
        zr = zr.transpose(1, 0, 3, 2).reshape(g, L * H, p)
        zi = zi.transpose(1, 0, 3, 2).reshape(g, L * H, p)
        return jnp.concatenate([zr, zi], -1), jnp.concatenate([zi, zr], -1)

    in_f, insw_f = m_in(0, pwr[:L, 0][::-1], pwi[:L, 0][::-1])
    in_r, insw_r = m_in(1, pwr[:L, 1], pwi[:L, 1])

    def m_out(d, er, ei):
        vr, vi = _cmul(cr[d][None], ci[d][None], er[:, :, None, :], ei[:, :, None, :])
        vr = vr.transpose(1, 3, 0, 2).reshape(g, p, L * H)
        vi = vi.transpose(1, 3, 0, 2).reshape(g, p, L * H)
        return jnp.concatenate([vr, -vi], 1)

    out_f = m_out(0, pwr[1:, 0], pwi[1:, 0])
    out_r = m_out(1, pwr[1:, 1][::-1], pwi[1:, 1][::-1])

    def a_rows(d):
        alr, ali = pwr[L, d], pwi[L, d]
        return [jnp.concatenate([alr, alr], -1),
                jnp.concatenate([-ali, ali], -1),
                jnp.concatenate([ali, -ali], -1)]

    rows = a_rows(0) + a_rows(1)
    rows = rows + [jnp.zeros_like(rows[0])] * 2
    a_tab = jnp.stack(rows, 1)
    w_all = jnp.concatenate([m_f, m_r, in_f, insw_f, in_r, insw_r], -1).astype(BF16)
    return w_all, jnp.stack([out_f, out_r], 1).astype(BF16), a_tab


def _s5_kernel(u_ref, w_ref, mo_ref, a_ref, y_ref, z_ref, sf_ref, sr_ref, *, nb, n_ctx, n_lat):
    lh = SSM_CHUNK * SSM_GROUP
    u = u_ref[0]
    z_ref[...] = jnp.dot(u, w_ref[0, :, 2 * lh:], preferred_element_type=F32)
    a = a_ref[0]
    n_all = n_ctx + n_lat

    def run(col, a1, a2, a2sw, order, s_ref, steps, carry):
        def body(kk, c):
            s, ssw = c
            r0 = pl.multiple_of(order(kk) * nb, nb)
            s_ref[pl.ds(r0, nb), :] = s
            x = z_ref[pl.ds(r0, nb), col:col + LANES]
            xsw = z_ref[pl.ds(r0, nb), col + LANES:col + 2 * LANES]
            return a1 * s + a2 * ssw + x, a1 * ssw + a2sw * s + xsw
        return lax.fori_loop(0, steps, body, carry, unroll=8)

    zero = jnp.zeros((nb, LANES), F32)
    run(0, a[0:1], a[1:2], a[2:3], lambda kk: kk, sf_ref, n_all, (zero, zero))
    c = run(2 * LANES, a[3:4], a[4:5], a[5:6], lambda kk: n_ctx - 1 - kk, sr_ref, n_ctx, (zero, zero))
    run(2 * LANES, a[3:4], a[4:5], a[5:6], lambda kk: n_all - 1 - kk, sr_ref, n_lat, c)

    lat0 = n_ctx * nb
    ul = u_ref[0, lat0:, :]
    yi = jnp.dot(ul, w_ref[0, :, :2 * lh], preferred_element_type=F32)
    y = yi[:, :lh] + yi[:, lh:]
    y = y + jnp.dot(sf_ref[lat0:, :].astype(BF16), mo_ref[0, 0], preferred_element_type=F32)
    y = y + jnp.dot(sr_ref[lat0:, :].astype(BF16), mo_ref[0, 1], preferred_element_type=F32)
    y_ref[0] = y


def _s5(u_all, w_all, m_out, a_tab, nb, n_ctx, n_lat):
    g, r, lh = u_all.shape
    rl = n_lat * nb
    return pl.pallas_call(
        functools.partial(_s5_kernel, nb=nb, n_ctx=n_ctx, n_lat=n_lat),
        grid=(g,),
        in_specs=[pl.BlockSpec((1, r, lh), lambda i: (i, 0, 0)),
                  pl.BlockSpec((1, lh, 4 * lh), lambda i: (i, 0, 0)),
                  pl.BlockSpec((1, 2, LANES, lh), lambda i: (i, 0, 0, 0)),
                  pl.BlockSpec((1, 8, LANES), lambda i: (i, 0, 0))],
        out_specs=pl.BlockSpec((1, rl, lh), lambda i: (i, 0, 0)),
        out_shape=jax.ShapeDtypeStruct((g, rl, lh), F32),
        scratch_shapes=[pltpu.VMEM((r, 4 * LANES), F32),
                        pltpu.VMEM((r, LANES), F32),
                        pltpu.VMEM((r, LANES), F32)],
        compiler_params=_cparams(("parallel",)),
        name="s5_scan",
    )(u_all, w_all, m_out, a_tab)


def _to_groups(u):
    b, n, w = u.shape
    g = w // SSM_GROUP
    return (u.reshape(b, n // SSM_CHUNK, SSM_CHUNK, g, SSM_GROUP)
            .transpose(3, 1, 0, 2, 4).reshape(g, n // SSM_CHUNK, b, SSM_CHUNK * SSM_GROUP))


def _na_bias(rpb):
    kh = NA_ROWS_MAX
    cols = np.arange(GRID_W)
    c0 = np.clip(cols - NA_COLS // 2, 0, GRID_W - NA_COLS)
    col_in = (cols[None, :] >= c0[:, None]) & (cols[None, :] < c0[:, None] + NA_COLS)
    dc = np.clip(cols[None, :] - cols[:, None] + NA_COLS - 1, 0, 2 * NA_COLS - 2)
    dr = np.arange(kh)[None, :] - np.arange(kh)[:, None] + NA_ROWS_MAX - 1
    pick_r = (dr[:, :, None] == np.arange(2 * NA_ROWS_MAX - 1)).astype(np.float32)
    pick_c = (dc[:, :, None] == np.arange(2 * NA_COLS - 1)).astype(np.float32)
    hp = lax.Precision.HIGHEST
    rows = jnp.einsum('hrc,vnr->vhnc', rpb.astype(F32), pick_r, precision=hp)
    bias = jnp.einsum('vhnc,qkc->vhqnk', rows, pick_c, precision=hp)
    bias = jnp.where(col_in[None, None, :, None, :], bias, NEG)
    heads = rpb.shape[0]
    return bias.reshape(kh, heads, GRID_W, kh * GRID_W)


def _na_kernel(q_ref, k_ref, v_ref, kc_ref, vc_ref, bias_ref, o_ref, *, rows, kh):
    r = pl.program_id(1)
    r0 = jnp.clip(r - kh // 2, 0, rows - kh)
    start = pl.multiple_of(r0 * GRID_W, GRID_W)
    nband = kh * GRID_W
    lane = lax.broadcasted_iota(jnp.int32, (1, LANES), 1)
    lo = lane < NA_HEAD_DIM
    scale = NA_HEAD_DIM ** -0.5
    nt = (((1,), (1,)), ((), ()))
    n_pairs = q_ref.shape[2] // LANES
    for hp in range(n_pairs):
        cs = slice(hp * LANES, (hp + 1) * LANES)
        q2 = q_ref[0, :, cs]
        k2 = k_ref[0, pl.ds(start, nband), cs]
        v2 = v_ref[0, pl.ds(start, nband), cs]
        kc2 = kc_ref[0, :, cs]
        vc2 = vc_ref[0, :, cs]
        zq = jnp.zeros_like(q2)
        qm = jnp.concatenate([jnp.where(lo, q2, zq), jnp.where(lo, zq, q2)], axis=0)
        bias = jnp.concatenate([bias_ref[0, 2 * hp], bias_ref[0, 2 * hp + 1]], axis=0)
        s = lax.dot_general(qm, k2, nt, preferred_element_type=F32) * scale + bias
        sc = lax.dot_general(qm, kc2, nt, preferred_element_type=F32) * scale
        m = jnp.maximum(jnp.max(s, axis=-1, keepdims=True), jnp.max(sc, axis=-1, keepdims=True))
        p = jnp.exp(s - m)
        pc = jnp.exp(sc - m)
        den = jnp.sum(p, axis=-1, keepdims=True) + jnp.sum(pc, axis=-1, keepdims=True)
        o = (jnp.dot(p.astype(BF16), v2, preferred_element_type=F32)
             + jnp.dot(pc.astype(BF16), vc2, preferred_element_type=F32)) / den
        o_ref[0, :, cs] = jnp.where(lo, o[:GRID_W], o[GRID_W:]).astype(o_ref.dtype)


def _na(px, pc, bias, width, col_k, col_v, col_q):
    b, n, _ = px.shape
    n_ctx = pc.shape[1]
    rows = n // GRID_W
    kh = min(NA_ROWS_MAX, rows)
    assert kh == NA_ROWS_MAX, "sequence must span at least NA_ROWS_MAX grid rows"
    heads = bias.shape[1]

    def bias_map(i, r):
        return (r - jnp.clip(r - kh // 2, 0, rows - kh), 0, 0, 0)

    return pl.pallas_call(
        functools.partial(_na_kernel, rows=rows, kh=kh),
        grid=(b, rows),
        in_specs=[pl.BlockSpec((1, GRID_W, width), lambda i, r: (i, r, col_q // width)),
                  pl.BlockSpec((1, n, width), lambda i, r: (i, 0, col_k // width)),
                  pl.BlockSpec((1, n, width), lambda i, r: (i, 0, col_v // width)),
                  pl.BlockSpec((1, n_ctx, width), lambda i, r: (i, 0, col_k // width)),
                  pl.BlockSpec((1, n_ctx, width), lambda i, r: (i, 0, col_v // width)),
                  pl.BlockSpec((1, heads, GRID_W, kh * GRID_W), bias_map)],
        out_specs=pl.BlockSpec((1, GRID_W, width), lambda i, r: (i, r, 0)),
        out_shape=jax.ShapeDtypeStruct((b, n, width), BF16),
        compiler_params=_cparams(("parallel", "arbitrary")),
        name="na_mixer",
    )(px, px, px, pc, pc, bias)


def _merge_kernel(x_ref, y_ref, o_ref, ga_ref, gb_ref, gate_ref, shift_ref, scale_ref, g2_ref,
                  wglu_ref, bglu_ref, wa_ref, wb_ref, wo_ref, x1_ref, h2t_ref):
    z = _gelu(y_ref[0])
    gl = jnp.dot(z.astype(BF16), wglu_ref[...], preferred_element_type=F32) + bglu_ref[...]
    zz = z * _sigmoid(gl)
    ba = jnp.dot(zz.astype(BF16), wa_ref[...], preferred_element_type=F32)
    bb = jnp.dot(o_ref[0], wb_ref[...], preferred_element_type=F32)
    merged = _sigmoid(ga_ref[0].astype(F32)) * ba + _sigmoid(gb_ref[0].astype(F32)) * bb
    x1 = x_ref[0] + gate_ref[0] * jnp.dot(merged.astype(BF16), wo_ref[...], preferred_element_type=F32)
    x1_ref[0] = x1
    h2 = _rms(x1, g2_ref[...]) * (1.0 + scale_ref[0]) + shift_ref[0]
    h2t_ref[...] = h2.T.astype(h2t_ref.dtype)


def _merge(x, y_ssm, o_na, px, gate1, shift2, scale2, g2, w_glu, b_glu, w_a, w_b, w_o, col_ga, col_gb, tm):
    b, n, d = x.shape
    w = y_ssm.shape[2]
    tok = lambda i, j: (i, j, 0)
    per_b = lambda i, j: (i, 0, 0)
    full = lambda i, j: (0, 0)
    return pl.pallas_call(
        _merge_kernel,
        grid=(b, n // tm),
        in_specs=[pl.BlockSpec((1, tm, d), tok),
                  pl.BlockSpec((1, tm, w), tok),
                  pl.BlockSpec((1, tm, w), tok),
                  pl.BlockSpec((1, tm, d), lambda i, j: (i, j, col_ga // d)),
                  pl.BlockSpec((1, tm, d), lambda i, j: (i, j, col_gb // d)),
                  pl.BlockSpec((1, 1, d), per_b),
                  pl.BlockSpec((1, 1, d), per_b),
                  pl.BlockSpec((1, 1, d), per_b),
                  pl.BlockSpec((1, d), full),
                  pl.BlockSpec((w, w), full),
                  pl.BlockSpec((1, w), full),
                  pl.BlockSpec((w, d), full),
                  pl.BlockSpec((w, d), full),
                  pl.BlockSpec((d, d), full)],
        out_specs=[pl.BlockSpec((1, tm, d), tok),
                   pl.BlockSpec((d, tm), lambda i, j: (0, i * (n // tm) + j))],
        out_shape=[jax.ShapeDtypeStruct((b, n, d), F32), jax.ShapeDtypeStruct((d, b * n), BF16)],
        compiler_params=_cparams(("parallel", "parallel")),
        name="merge",
    )(x, y_ssm, o_na, px, px, gate1, shift2, scale2, g2, w_glu, b_glu, w_a, w_b, w_o)


CAND_B = (16, 8, 5, 4, 3, 2, 2, 2)


def _top16_exact(s):
    n = s.shape[0]
    iota = lax.broadcasted_iota(jnp.int32, s.shape, 0).astype(F32)
    cur = s
    rank = jnp.full(s.shape, 127.0, F32)
    vals = []
    for k in range(PEER_TOPK):
        m = jnp.max(cur, axis=0, keepdims=True)
        idx = jnp.min(jnp.where(cur == m, iota, float(n)), axis=0, keepdims=True)
        hit = iota == idx
        rank = jnp.where(hit, float(k), rank)
        cur = jnp.where(hit, -jnp.inf, cur)
        vals.append(m)
    return rank, jnp.concatenate(vals, axis=0)


def _top16_fast(s):
    cur = s
    vals = []
    for k in range(PEER_TOPK):
        m = jnp.max(cur, axis=0, keepdims=True)
        cur = jnp.where(cur == m, -MARK * (PEER_TOPK + k), cur)
        vals.append(m)
    marked = cur <= -MARK * PEER_TOPK
    rank = jnp.where(marked, cur * (-1.0 / MARK) - float(PEER_TOPK), 127.0)
    count = jnp.sum(jnp.where(marked, 1.0, 0.0), axis=0, keepdims=True)
    return rank, jnp.concatenate(vals, axis=0), count


def _cand_groups(v0, v1, combine, pad):
    sub = lax.broadcasted_iota(jnp.int32, (SUBLANES, v0.shape[1]), 0)
    first = combine(v0[0:1], v1)
    groups = [first[:SUBLANES], first[SUBLANES:]]
    for a in range(1, len(CAND_B)):
        piece = combine(v0[a:a + 1], v1[0:SUBLANES])
        groups.append(jnp.where(sub < CAND_B[a], piece, pad))
    groups.append(combine(v0[SUBLANES:], v1[0:1]))
    return groups


def _cand_list():
    out = [(b, 0, b) for b in range(CAND_B[0])]
    for a in range(1, len(CAND_B)):
        out += [(PEER_TOPK + SUBLANES * (a - 1) + b, a, b) for b in range(CAND_B[a])]
    base = PEER_TOPK + SUBLANES * (len(CAND_B) - 1)
    out += [(base + a - SUBLANES, a, 0) for a in range(SUBLANES, PEER_TOPK)]
    return out


def _select16(v0, v1):
    groups = _cand_groups(v0, v1, lambda x, y: x + y, -jnp.inf)
    sub = lax.broadcasted_iota(jnp.int32, groups[0].shape, 0)
    beaten = [jnp.zeros(groups[0].shape, F32) for _ in groups]
    for row, a, b in _cand_list():
        vc = v0[a:a + 1] + v1[b:b + 1]
        gc, rc = divmod(row, SUBLANES)
        for gi, grp in enumerate(groups):
            if gi == gc:
                first = jnp.where(sub > rc, jnp.where(vc >= grp, 1.0, 0.0), jnp.where(vc > grp, 1.0, 0.0))
                beaten[gi] = beaten[gi] + first
            else:
                first = (vc > grp) if gi < gc else (vc >= grp)
                beaten[gi] = jnp.where(first, beaten[gi] + 1.0, beaten[gi])
    return [jnp.where(bt < float(PEER_TOPK), 1.0, 0.0) for bt in beaten]


def _route_head(st_ref, re_ref, nc_ref, lt, hd, exact):
    s0 = st_ref[lt, 2 * hd]
    s1 = st_ref[lt, 2 * hd + 1]
    worst = None
    if exact:
        (rank0, v0), (rank1, v1) = _top16_exact(s0), _top16_exact(s1)
    else:
        rank0, v0, c0 = _top16_fast(s0)
        rank1, v1, c1 = _top16_fast(s1)
        worst = jnp.maximum(jnp.abs(c0 - float(PEER_TOPK)), jnp.abs(c1 - float(PEER_TOPK)))
    sel = _select16(v0, v1)
    e0 = jnp.exp(v0 - v0[0:1])
    e1 = jnp.exp(v1 - v1[0:1])
    prod = _cand_groups(e0, e1, lambda x, y: x * y, 0.0)
    zsum = None
    for sg, pg in zip(sel, prod):
        part = jnp.sum(sg * pg, axis=0, keepdims=True)
        zsum = part if zsum is None else zsum + part
    cnt = jnp.zeros(rank0.shape, F32)
    for a in range(PEER_TOPK):
        if a == 0:
            n_a = jnp.sum(sel[0] + sel[1], axis=0, keepdims=True)
        elif a < SUBLANES:
            n_a = jnp.sum(sel[1 + a], axis=0, keepdims=True)
        else:
            n_a = sel[-1][a - SUBLANES:a - SUBLANES + 1]
        cnt = jnp.where(rank0 == float(a), n_a, cnt)
    coef = jnp.where(rank0 < float(PEER_TOPK), jnp.exp(s0 - v0[0:1]) / zsum, 0.0)
    e1f = jnp.where(rank1 < float(PEER_TOPK), jnp.exp(s1 - v1[0:1]), 0.0)
    re_ref[0, hd] = rank1.astype(re_ref.dtype)
    re_ref[1, hd] = e1f.astype(re_ref.dtype)
    nc_ref[0, hd, 0] = cnt
    nc_ref[1, hd, 0] = coef
    return worst


def _route_kernel(ht_ref, wq_ref, sk_ref, re_ref, nc_ref, st_ref, *, heads):
    lt = pl.program_id(1)
    kq = sk_ref.shape[2]

    @pl.when(lt == 0)
    def _():
        qt = jnp.dot(wq_ref[...], ht_ref[...], preferred_element_type=F32).astype(BF16)
        for hn in range(2 * heads):
            st = jnp.dot(sk_ref[hn % 2], qt[hn * kq:(hn + 1) * kq], preferred_element_type=F32)
            for ti in range(st_ref.shape[0]):
                st_ref[ti, hn] = st[:, ti * LANES:(ti + 1) * LANES]

    def head_body(hd, carry):
        worst = _route_head(st_ref, re_ref, nc_ref, lt, hd, exact=False)

        @pl.when(jnp.max(worst) > 0.0)
        def _():
            _route_head(st_ref, re_ref, nc_ref, lt, hd, exact=True)

        return carry

    lax.fori_loop(0, heads, head_body, 0)


def _route(h2t, wq_t, subkeys, heads, tm):
    d, t = h2t.shape
    keys = subkeys.shape[1]
    n_lt = tm // LANES
    assert keys == LANES
    return pl.pallas_call(
        functools.partial(_route_kernel, heads=heads),
        grid=(t // tm, n_lt),
        in_specs=[pl.BlockSpec((d, tm), lambda i, l: (0, i)),
                  pl.BlockSpec(wq_t.shape, lambda i, l: (0, 0)),
                  pl.BlockSpec(subkeys.shape, lambda i, l: (0, 0, 0))],
        out_specs=[pl.BlockSpec((2, heads, keys, LANES), lambda i, l: (0, 0, 0, i * n_lt + l)),
                   pl.BlockSpec((2, heads, 1, keys, LANES), lambda i, l: (0, 0, i * n_lt + l, 0, 0))],
        out_shape=[jax.ShapeDtypeStruct((2, heads, keys, t), BF16),
                   jax.ShapeDtypeStruct((2, heads, t // LANES, keys, LANES), F32)],
        scratch_shapes=[pltpu.VMEM((n_lt, 2 * heads, keys, LANES), F32)],
        compiler_params=_cparams(("parallel", "arbitrary")),
        name="peer_route",
    )(h2t, wq_t, subkeys)


def _peer_kernel(ht_ref, u_ref, vt_ref, re_ref, nc_ref, x1_ref, gate_ref, fg_ref, o_ref,
                 acc_ref, a_ref, w_ref, *, heads):
    j = pl.program_id(1)
    pack = 2 * SUBLANES

    @pl.when(j == 0)
    def _():
        acc_ref[...] = jnp.zeros_like(acc_ref)

    a_ref[...] = jnp.dot(u_ref[...], ht_ref[...], preferred_element_type=F32)
    te, tm = a_ref.shape
    wc = PEER_COLS
    for sl in range(te // LANES):
        rs = slice(sl * LANES, (sl + 1) * LANES)
        for c0 in range(0, tm, wc):
            cs = slice(c0, c0 + wc)
            tiles = range(c0 // LANES, (c0 + wc) // LANES)
            g = None
            for hd in range(heads):
                nrow = jnp.concatenate([jnp.broadcast_to(nc_ref[0, hd, ti, sl:sl + 1, :], (pack, LANES))
                                        for ti in tiles], axis=1).astype(BF16)
                crow = jnp.concatenate([jnp.broadcast_to(nc_ref[1, hd, ti, sl:sl + 1, :], (pack, LANES))
                                        for ti in tiles], axis=1).astype(BF16)
                r1 = re_ref[0, hd, :, cs].reshape(LANES // pack, pack, wc)
                e1 = re_ref[1, hd, :, cs].reshape(LANES // pack, pack, wc)
                term = jnp.where(r1 < nrow[None], e1, jnp.zeros_like(e1)) * crow[None]
                g = term if g is None else g + term
            w_ref[rs, cs] = _gelu(a_ref[rs, cs]).astype(BF16) * g.reshape(LANES, wc)
    acc_ref[...] += jnp.dot(vt_ref[...], w_ref[...], preferred_element_type=F32)

    @pl.when(j == pl.num_programs(1) - 1)
    def _():
        x2 = x1_ref[...] + gate_ref[0] * acc_ref[...].T
        o_ref[...] = _rms(x2, fg_ref[...])


def _peer(h2t, u_tab, v_t, route_re, route_nc, x1, gate2, final_g, n_per_batch, tm, te):
    d, t = h2t.shape
    e = u_tab.shape[0]
    heads, keys = route_re.shape[1], route_re.shape[2]
    assert keys == LANES and te % (SUBLANES * LANES) == 0 and n_per_batch % tm == 0
    return pl.pallas_call(
        functools.partial(_peer_kernel, heads=heads),
        grid=(t // tm, e // te),
        in_specs=[pl.BlockSpec((d, tm), lambda i, j: (0, i)),
                  pl.BlockSpec((te, d), lambda i, j: (j, 0)),
                  pl.BlockSpec((d, te), lambda i, j: (0, j)),
                  pl.BlockSpec((2, heads, keys, tm), lambda i, j: (0, 0, 0, i)),
                  pl.BlockSpec((2, heads, tm // LANES, te // LANES, LANES), lambda i, j: (0, 0, i, j, 0)),
                  pl.BlockSpec((tm, d), lambda i, j: (i, 0)),
                  pl.BlockSpec((1, 1, d), lambda i, j: ((i * tm) // n_per_batch, 0, 0)),
                  pl.BlockSpec((1, d), lambda i, j: (0, 0))],
        out_specs=pl.BlockSpec((tm, d), lambda i, j: (i, 0)),
        out_shape=jax.ShapeDtypeStruct((t, d), F32),
        scratch_shapes=[pltpu.VMEM((d, tm), F32),
                        pltpu.VMEM((te, tm), F32),
                        pltpu.VMEM((te, tm), BF16)],
        compiler_params=_cparams(("parallel", "arbitrary")),
        name="peer_dense",
    )(h2t, u_tab, v_t, route_re, route_nc, x1, gate2, final_g)


def _layer(x, ctx, c, c_ctx, p, final_g):
    b, n, d = x.shape
    n_ctx = ctx.shape[1]
    width = p['w_glu'].shape[0]
    col_k, col_v, col_q = width, 2 * width, 3 * width
    col_ga, col_gb = 4 * width, 4 * width + d
    heads_p = p['peer_w_q'].shape[1] // (2 * p['peer_subkeys'].shape[2])
    assert b % SUBLANES == 0 and n % (GRID_W * SSM_CHUNK) == 0 and n_ctx % SSM_CHUNK == 0
    assert width % LANES == 0 and col_ga % d == 0

    rows = 2 * SUBLANES
    cond = jnp.zeros((rows, d), F32).at[:b].set(c).at[b].set(c_ctx)
    mod = _adaln(cond, p['w_mod'], p['b_mod'])
    chunks = [mod[:, i * d:(i + 1) * d] for i in range(6)]
    shift1, scale1, gate1, shift2, scale2, gate2 = [m[:b, None, :] for m in chunks]
    shift_c, scale_c = chunks[0][b:b + 1, None, :], chunks[1][b:b + 1, None, :]

    w_in = p['w_in'].astype(BF16)
    g1 = p['norm1_g'].reshape(1, d)
    px = _inproj(x, g1, shift1, scale1, w_in, tm=256)
    pc = _inproj(ctx, g1, shift_c, scale_c, w_in[:, :col_q], tm=n_ctx)

    w_all, m_out, a_tab = _s5_tables(p['ssm_a_re'], p['ssm_a_im'], p['ssm_log_dt'], p['ssm_b_re'],
                                     p['ssm_b_im'], p['ssm_c_re'], p['ssm_c_im'], p['ssm_d'])
    cc, cl = n_ctx // SSM_CHUNK, n // SSM_CHUNK
    u_all = jnp.concatenate([_to_groups(pc[..., :width]), _to_groups(px[..., :width])], axis=1)
    groups = u_all.shape[0]
    u_all = u_all.reshape(groups, (cc + cl) * b, SSM_CHUNK * SSM_GROUP)
    y_g = _s5(u_all, w_all, m_out, a_tab, b, cc, cl)
    y_ssm = (y_g.reshape(groups, cl, b, SSM_CHUNK, SSM_GROUP)
             .transpose(2, 1, 3, 0, 4).reshape(b, n, width))

    o_na = _na(px, pc, _na_bias(p['na_rpb']), width, col_k, col_v, col_q)

    x1, h2t = _merge(x, y_ssm, o_na, px, gate1, shift2, scale2, p['norm2_g'].reshape(1, d),
                     p['w_glu'].astype(BF16), p['b_glu'].reshape(1, width),
                     p['w_branch_a'].astype(BF16), p['w_branch_b'].astype(BF16),
                     p['w_out'].astype(BF16), col_ga, col_gb, tm=256)

    t = b * n
    route_re, route_nc = _route(h2t, p['peer_w_q'].T.astype(BF16), p['peer_subkeys'].astype(BF16),
                                heads_p, tm=512)
    out = _peer(h2t, p['peer_u'].astype(BF16), p['peer_v'].T.astype(BF16), route_re, route_nc,
                x1.reshape(t, d), gate2, final_g.reshape(1, d), n, tm=512, te=1024)
    return out.reshape(b, n, d)


def kernel(x, c, ctx, c_ctx, w_mod, b_mod, norm1_g, norm2_g, w_in, ssm_a_re, ssm_a_im, ssm_log_dt,
           ssm_b_re, ssm_b_im, ssm_c_re, ssm_c_im, ssm_d, w_glu, b_glu, w_branch_a, w_branch_b, na_rpb,
           w_out, peer_w_q, peer_subkeys, peer_u, peer_v, final_g):
    assert w_mod.shape[0] == 1, "single-layer stack"
    p = {
        'w_mod': w_mod[0], 'b_mod': b_mod[0], 'norm1_g': norm1_g[0], 'norm2_g': norm2_g[0],
        'w_in': w_in[0],
        'ssm_a_re': ssm_a_re[0], 'ssm_a_im': ssm_a_im[0], 'ssm_log_dt': ssm_log_dt[0],
        'ssm_b_re': ssm_b_re[0], 'ssm_b_im': ssm_b_im[0], 'ssm_c_re': ssm_c_re[0], 'ssm_c_im': ssm_c_im[0],
        'ssm_d': ssm_d[0], 'w_glu': w_glu[0], 'b_glu': b_glu[0],
        'w_branch_a': w_branch_a[0], 'w_branch_b': w_branch_b[0], 'na_rpb': na_rpb[0], 'w_out': w_out[0],
        'peer_w_q': peer_w_q[0], 'peer_subkeys': peer_subkeys[0], 'peer_u': peer_u[0], 'peer_v': peer_v[0],
    }
    return _layer(x, ctx, c, c_ctx, p, final_g)
```

```python
import functools
import math

import jax
import jax.numpy as jnp
import numpy as np
from jax import lax
from jax.experimental import pallas as pl
from jax.experimental.pallas import tpu as pltpu

EPS = 1e-6
F32 = jnp.float32
BF16 = jnp.bfloat16

SSM_GROUP = 16
SSM_CHUNK = 16
NA_HEAD_DIM = 64
NA_ROWS_MAX = 8
NA_COLS = 16
NA_ROWS_PER_STEP = 2
GRID_W = 64
PEER_TOPK = 16
LANES = 128
SUBLANES = 8
NEG = -1e30
MARK = 2.0 ** 100
PEER_COLS = 256
PEER_SUB = 1024
ROUTE_HEADS_PER_ITER = 2

VMEM_LIMIT = 56 * 1024 * 1024


def _cparams(sem):
    return pltpu.CompilerParams(dimension_semantics=sem, vmem_limit_bytes=VMEM_LIMIT)


def _sigmoid(x):
    return 1.0 / (1.0 + jnp.exp(-x))


_GELU_K1 = -2.0 * math.sqrt(2.0 / math.pi) * math.log2(math.e)
_GELU_K2 = _GELU_K1 * 0.044715


def _gelu(x):
    t = x * (_GELU_K1 + _GELU_K2 * (x * x))
    return x / (1.0 + jnp.exp2(t))


def _rms(x, g):
    return x * lax.rsqrt(jnp.mean(x * x, axis=-1, keepdims=True) + EPS) * g


def _adaln_kernel(c_ref, w_ref, b_ref, o_ref):
    c = c_ref[...]
    a = c * _sigmoid(c)
    o_ref[...] = jnp.dot(a, w_ref[...], preferred_element_type=F32,
                         precision=lax.Precision.HIGHEST) + b_ref[...]


def _adaln(cond, w_mod, b_mod):
    rows, d = cond.shape
    cols = w_mod.shape[1]
    tn = 1536
    return pl.pallas_call(
        _adaln_kernel,
        grid=(cols // tn,),
        in_specs=[pl.BlockSpec((rows, d), lambda j: (0, 0)),
                  pl.BlockSpec((d, tn), lambda j: (0, j)),
                  pl.BlockSpec((1, tn), lambda j: (0, j))],
        out_specs=pl.BlockSpec((rows, tn), lambda j: (0, j)),
        out_shape=jax.ShapeDtypeStruct((rows, cols), F32),
        compiler_params=_cparams(("arbitrary",)),
        name="adaln",
    )(cond, w_mod, b_mod.reshape(1, cols))


def _inproj_kernel(x_ref, g_ref, shift_ref, scale_ref, w_ref, o_ref):
    h = _rms(x_ref[0], g_ref[...]) * (1.0 + scale_ref[0]) + shift_ref[0]
    o_ref[0] = jnp.dot(h.astype(BF16), w_ref[...], preferred_element_type=F32).astype(o_ref.dtype)


def _inproj(x, g, shift, scale, w, tm):
    b, n, d = x.shape
    cols = w.shape[1]
    per_batch = shift.shape[0] == b
    mod_map = (lambda i, j: (i, 0, 0)) if per_batch else (lambda i, j: (0, 0, 0))
    return pl.pallas_call(
        _inproj_kernel,
        grid=(b, n // tm),
        in_specs=[pl.BlockSpec((1, tm, d), lambda i, j: (i, j, 0)),
                  pl.BlockSpec((1, d), lambda i, j: (0, 0)),
                  pl.BlockSpec((1, 1, d), mod_map),
                  pl.BlockSpec((1, 1, d), mod_map),
                  pl.BlockSpec((d, cols), lambda i, j: (0, 0))],
        out_specs=pl.BlockSpec((1, tm, cols), lambda i, j: (i, j, 0)),
        out_shape=jax.ShapeDtypeStruct((b, n, cols), BF16),
        compiler_params=_cparams(("parallel", "parallel")),
        name="inproj",
    )(x, g, shift, scale, w)


def _cmul(ar, ai, br, bi):
    return ar * br - ai * bi, ar * bi + ai * br


def _s5_tables(a_re, a_im, log_dt, b_re, b_im, c_re, c_im, d_skip):
    L, H = SSM_CHUNK, SSM_GROUP
    hp = lax.Precision.HIGHEST
    ar, ai = a_re.astype(F32), a_im.astype(F32)
    dt = jnp.exp(log_dt.astype(F32))[..., None]
    xr, xi = ar * dt, ai * dt
    k = jnp.arange(L + 1, dtype=F32)[:, None, None, None]
    mag = jnp.exp(k * xr[None])
    pwr, pwi = mag * jnp.cos(k * xi[None]), mag * jnp.sin(k * xi[None])
    den = ar * ar + ai * ai
    ur, ui = pwr[1] - 1.0, pwi[1]
    qr, qi = (ur * ar + ui * ai) / den, (ui * ar - ur * ai) / den
    bbr, bbi = _cmul(qr[..., None], qi[..., None], b_re.astype(F32), b_im.astype(F32))
    cr, ci = c_re.astype(F32), c_im.astype(F32)
    g, p = ar.shape[1], ar.shape[2]

    wr, wi = _cmul(cr[None], ci[None], pwr[:L, :, :, None, :], pwi[:L, :, :, None, :])
    kern = (jnp.einsum('tdghp,dgpk->tdghk', wr, bbr, precision=hp)
            - jnp.einsum('tdghp,dgpk->tdghk', wi, bbi, precision=hp))
    t_idx = np.arange(L)
    lag_f = t_idx[None, :] - t_idx[:, None]
    sel_f = (lag_f[None] == t_idx[:, None, None]).astype(np.float32)
    sel_r = (-lag_f[None] == t_idx[:, None, None]).astype(np.float32)
    m_f = jnp.einsum('sjt,sghk->gjkth', sel_f, kern[:, 0], precision=hp)
    m_r = jnp.einsum('sjt,sghk->gjkth', sel_r, kern[:, 1], precision=hp)
    skip = d_skip.astype(F32).reshape(g, H)
    eye_l, eye_h = np.eye(L, dtype=np.float32), np.eye(H, dtype=np.float32)
    m_f = m_f + eye_l[None, :, None, :, None] * eye_h[None, None, :, None, :] * skip[:, None, None, None, :]
    m_f = m_f.reshape(g, L * H, L * H)
    m_r = m_r.reshape(g, L * H, L * H)

    def m_in(d, er, ei):
        zr, zi = _cmul(er[..., None], ei[..., None], bbr[d][None], bbi[d][None])
        zr = zr.transpose(1, 0, 3, 2).reshape(g, L * H, p)
        zi = zi.transpose(1, 0, 3, 2).reshape(g, L * H, p)
        return jnp.concatenate([zr, zi], -1), jnp.concatenate([zi, zr], -1)

    in_f, insw_f = m_in(0, pwr[:L, 0][::-1], pwi[:L, 0][::-1])
    in_r, insw_r = m_in(1, pwr[:L, 1], pwi[:L, 1])

    def m_out(d, er, ei):
        vr, vi = _cmul(cr[d][None], ci[d][None], er[:, :, None, :], ei[:, :, None, :])
        vr = vr.transpose(1, 3, 0, 2).reshape(g, p, L * H)
        vi = vi.transpose(1, 3, 0, 2).reshape(g, p, L * H)
        return jnp.concatenate([vr, -vi], 1)

    out_f = m_out(0, pwr[1:, 0], pwi[1:, 0])
    out_r = m_out(1, pwr[1:, 1][::-1], pwi[1:, 1][::-1])

    def a_rows(d):
        alr, ali = pwr[L, d], pwi[L, d]
        return [jnp.concatenate([alr, alr], -1),
                jnp.concatenate([-ali, ali], -1),
                jnp.concatenate([ali, -ali], -1)]

    rows = a_rows(0) + a_rows(1)
    rows = rows + [jnp.zeros_like(rows[0])] * 2
    a_tab = jnp.stack(rows, 1)
    w_all = jnp.concatenate([m_f, m_r, in_f, insw_f, in_r, insw_r], -1).astype(BF16)
    return w_all, jnp.stack([out_f, out_r], 1).astype(BF16), a_tab


def _s5_kernel(u_ref, w_ref, mo_ref, a_ref, y_ref, z_ref, sf_ref, sr_ref, *, nb, n_ctx, n_lat):
    lh = SSM_CHUNK * SSM_GROUP
    u = u_ref[0]
    z_ref[...] = jnp.dot(u, w_ref[0, :, 2 * lh:], preferred_element_type=F32)
    a = a_ref[0]
    n_all = n_ctx + n_lat

    def run(col, a1, a2, a2sw, order, s_ref, steps, carry):
        def body(kk, c):
            s, ssw = c
            r0 = pl.multiple_of(order(kk) * nb, nb)
            s_ref[pl.ds(r0, nb), :] = s
            x = z_ref[pl.ds(r0, nb), col:col + LANES]
            xsw = z_ref[pl.ds(r0, nb), col + LANES:col + 2 * LANES]
            return a1 * s + a2 * ssw + x, a1 * ssw + a2sw * s + xsw
        return lax.fori_loop(0, steps, body, carry, unroll=8)

    zero = jnp.zeros((nb, LANES), F32)
    run(0, a[0:1], a[1:2], a[2:3], lambda kk: kk, sf_ref, n_all, (zero, zero))
    c = run(2 * LANES, a[3:4], a[4:5], a[5:6], lambda kk: n_ctx - 1 - kk, sr_ref, n_ctx, (zero, zero))
    run(2 * LANES, a[3:4], a[4:5], a[5:6], lambda kk: n_all - 1 - kk, sr_ref, n_lat, c)

    lat0 = n_ctx * nb
    ul = u_ref[0, lat0:, :]
    yi = jnp.dot(ul, w_ref[0, :, :2 * lh], preferred_element_type=F32)
    y = yi[:, :lh] + yi[:, lh:]
    y = y + jnp.dot(sf_ref[lat0:, :].astype(BF16), mo_ref[0, 0], preferred_element_type=F32)
    y = y + jnp.dot(sr_ref[lat0:, :].astype(BF16), mo_ref[0, 1], preferred_element_type=F32)
    y_ref[0] = y.astype(y_ref.dtype)


def _s5(u_all, w_all, m_out, a_tab, nb, n_ctx, n_lat):
    g, r, lh = u_all.shape
    rl = n_lat * nb
    return pl.pallas_call(
        functools.partial(_s5_kernel, nb=nb, n_ctx=n_ctx, n_lat=n_lat),
        grid=(g,),
        in_specs=[pl.BlockSpec((1, r, lh), lambda i: (i, 0, 0)),
                  pl.BlockSpec((1, lh, 4 * lh), lambda i: (i, 0, 0)),
                  pl.BlockSpec((1, 2, LANES, lh), lambda i: (i, 0, 0, 0)),
                  pl.BlockSpec((1, 8, LANES), lambda i: (i, 0, 0))],
        out_specs=pl.BlockSpec((1, rl, lh), lambda i: (i, 0, 0)),
        out_shape=jax.ShapeDtypeStruct((g, rl, lh), BF16),
        scratch_shapes=[pltpu.VMEM((r, 4 * LANES), F32),
                        pltpu.VMEM((r, LANES), F32),
                        pltpu.VMEM((r, LANES), F32)],
        compiler_params=_cparams(("parallel",)),
        name="s5_scan",
    )(u_all, w_all, m_out, a_tab)


def _to_groups(u):
    b, n, w = u.shape
    g = w // SSM_GROUP
    return (u.reshape(b, n // SSM_CHUNK, SSM_CHUNK, g, SSM_GROUP)
            .transpose(3, 1, 0, 2, 4).reshape(g, n // SSM_CHUNK, b, SSM_CHUNK * SSM_GROUP))


def _na_bias(rpb):
    kh = NA_ROWS_MAX
    cols = np.arange(GRID_W)
    c0 = np.clip(cols - NA_COLS // 2, 0, GRID_W - NA_COLS)
    col_in = (cols[None, :] >= c0[:, None]) & (cols[None, :] < c0[:, None] + NA_COLS)
    dc = np.clip(cols[None, :] - cols[:, None] + NA_COLS - 1, 0, 2 * NA_COLS - 2)
    dr = np.arange(kh)[None, :] - np.arange(kh)[:, None] + NA_ROWS_MAX - 1
    pick_r = (dr[:, :, None] == np.arange(2 * NA_ROWS_MAX - 1)).astype(np.float32)
    pick_c = (dc[:, :, None] == np.arange(2 * NA_COLS - 1)).astype(np.float32)
    hp = lax.Precision.HIGHEST
    rows = jnp.einsum('hrc,vnr->vhnc', rpb.astype(F32), pick_r, precision=hp)
    bias = jnp.einsum('vhnc,qkc->vhqnk', rows, pick_c, precision=hp)
    bias = jnp.where(col_in[None, None, :, None, :], bias, NEG)
    heads = rpb.shape[0]
    return bias.reshape(kh, heads, GRID_W, kh * GRID_W)


def _na_kernel(q_ref, k_ref, v_ref, kc_ref, vc_ref, bias_ref, o_ref, *, rows, kh):
    nband = kh * GRID_W
    lane = lax.broadcasted_iota(jnp.int32, (1, LANES), 1)
    lo = lane < NA_HEAD_DIM
    scale = NA_HEAD_DIM ** -0.5
    nt = (((1,), (1,)), ((), ()))
    n_pairs = q_ref.shape[2] // LANES
    for rr in range(NA_ROWS_PER_STEP):
        r = pl.program_id(1) * NA_ROWS_PER_STEP + rr
        r0 = jnp.clip(r - kh // 2, 0, rows - kh)
        start = pl.multiple_of(r0 * GRID_W, GRID_W)
        var = r - r0
        qs = slice(rr * GRID_W, (rr + 1) * GRID_W)
        for hp in range(n_pairs):
            cs = slice(hp * LANES, (hp + 1) * LANES)
            q2 = q_ref[0, qs, cs]
            k2 = k_ref[0, pl.ds(start, nband), cs]
            v2 = v_ref[0, pl.ds(start, nband), cs]
            kc2 = kc_ref[0, :, cs]
            vc2 = vc_ref[0, :, cs]
            zq = jnp.zeros_like(q2)
            qm = jnp.concatenate([jnp.where(lo, q2, zq), jnp.where(lo, zq, q2)], axis=0)
            bias = jnp.concatenate([bias_ref[var, 2 * hp], bias_ref[var, 2 * hp + 1]], axis=0)
            s = lax.dot_general(qm, k2, nt, preferred_element_type=F32) * scale + bias
            sc = lax.dot_general(qm, kc2, nt, preferred_element_type=F32) * scale
            m = jnp.maximum(jnp.max(s, axis=-1, keepdims=True), jnp.max(sc, axis=-1, keepdims=True))
            p = jnp.exp(s - m)
            pc = jnp.exp(sc - m)
            den = jnp.sum(p, axis=-1, keepdims=True) + jnp.sum(pc, axis=-1, keepdims=True)
            o = (jnp.dot(p.astype(BF16), v2, preferred_element_type=F32)
                 + jnp.dot(pc.astype(BF16), vc2, preferred_element_type=F32)) / den
            o_ref[0, qs, cs] = jnp.where(lo, o[:GRID_W], o[GRID_W:]).astype(o_ref.dtype)


def _na(px, pc, bias, width, col_k, col_v, col_q):
    b, n, _ = px.shape
    n_ctx = pc.shape[1]
    rows = n // GRID_W
    kh = min(NA_ROWS_MAX, rows)
    assert kh == NA_ROWS_MAX, "sequence must span at least NA_ROWS_MAX grid rows"
    assert rows % NA_ROWS_PER_STEP == 0
    tq = NA_ROWS_PER_STEP * GRID_W
    return pl.pallas_call(
        functools.partial(_na_kernel, rows=rows, kh=kh),
        grid=(b, rows // NA_ROWS_PER_STEP),
        in_specs=[pl.BlockSpec((1, tq, width), lambda i, r: (i, r, col_q // width)),
                  pl.BlockSpec((1, n, width), lambda i, r: (i, 0, col_k // width)),
                  pl.BlockSpec((1, n, width), lambda i, r: (i, 0, col_v // width)),
                  pl.BlockSpec((1, n_ctx, width), lambda i, r: (i, 0, col_k // width)),
                  pl.BlockSpec((1, n_ctx, width), lambda i, r: (i, 0, col_v // width)),
                  pl.BlockSpec(bias.shape, lambda i, r: (0, 0, 0, 0))],
        out_specs=pl.BlockSpec((1, tq, width), lambda i, r: (i, r, 0)),
        out_shape=jax.ShapeDtypeStruct((b, n, width), BF16),
        compiler_params=_cparams(("parallel", "arbitrary")),
        name="na_mixer",
    )(px, px, px, pc, pc, bias)


def _merge_kernel(x_ref, y_ref, o_ref, ga_ref, gb_ref, gate_ref, shift_ref, scale_ref, g2_ref,
                  wglu_ref, bglu_ref, wa_ref, wb_ref, wo_ref, x1_ref, h2t_ref):
    z = _gelu(y_ref[0].astype(F32))
    gl = jnp.dot(z.astype(BF16), wglu_ref[...], preferred_element_type=F32) + bglu_ref[...]
    zz = z * _sigmoid(gl)
    ba = jnp.dot(zz.astype(BF16), wa_ref[...], preferred_element_type=F32)
    bb = jnp.dot(o_ref[0], wb_ref[...], preferred_element_type=F32)
    merged = _sigmoid(ga_ref[0].astype(F32)) * ba + _sigmoid(gb_ref[0].astype(F32)) * bb
    x1 = x_ref[0] + gate_ref[0] * jnp.dot(merged.astype(BF16), wo_ref[...], preferred_element_type=F32)
    x1_ref[0] = x1
    h2 = _rms(x1, g2_ref[...]) * (1.0 + scale_ref[0]) + shift_ref[0]
    h2t_ref[...] = h2.T.astype(h2t_ref.dtype)


def _merge(x, y_ssm, o_na, px, gate1, shift2, scale2, g2, w_glu, b_glu, w_a, w_b, w_o, col_ga, col_gb, tm):
    b, n, d = x.shape
    w = y_ssm.shape[2]
    tok = lambda i, j: (i, j, 0)
    per_b = lambda i, j: (i, 0, 0)
    full = lambda i, j: (0, 0)
    return pl.pallas_call(
        _merge_kernel,
        grid=(b, n // tm),
        in_specs=[pl.BlockSpec((1, tm, d), tok),
                  pl.BlockSpec((1, tm, w), tok),
                  pl.BlockSpec((1, tm, w), tok),
                  pl.BlockSpec((1, tm, d), lambda i, j: (i, j, col_ga // d)),
                  pl.BlockSpec((1, tm, d), lambda i, j: (i, j, col_gb // d)),
                  pl.BlockSpec((1, 1, d), per_b),
                  pl.BlockSpec((1, 1, d), per_b),
                  pl.BlockSpec((1, 1, d), per_b),
                  pl.BlockSpec((1, d), full),
                  pl.BlockSpec((w, w), full),
                  pl.BlockSpec((1, w), full),
                  pl.BlockSpec((w, d), full),
                  pl.BlockSpec((w, d), full),
                  pl.BlockSpec((d, d), full)],
        out_specs=[pl.BlockSpec((1, tm, d), tok),
                   pl.BlockSpec((d, tm), lambda i, j: (0, i * (n // tm) + j))],
        out_shape=[jax.ShapeDtypeStruct((b, n, d), F32), jax.ShapeDtypeStruct((d, b * n), BF16)],
        compiler_params=_cparams(("parallel", "parallel")),
        name="merge",
    )(x, y_ssm, o_na, px, px, gate1, shift2, scale2, g2, w_glu, b_glu, w_a, w_b, w_o)


CAND_B = (16, 8, 5, 4, 3, 2, 2, 2)


def _top16_exact(s):
    n = s.shape[0]
    iota = lax.broadcasted_iota(jnp.int32, s.shape, 0).astype(F32)
    cur = s
    rank = jnp.full(s.shape, 127.0, F32)
    vals = []
    for k in range(PEER_TOPK):
        m = jnp.max(cur, axis=0, keepdims=True)
        idx = jnp.min(jnp.where(cur == m, iota, float(n)), axis=0, keepdims=True)
        hit = iota == idx
        rank = jnp.where(hit, float(k), rank)
        cur = jnp.where(hit, -jnp.inf, cur)
        vals.append(m)
    return rank, jnp.concatenate(vals, axis=0)


def _top16_fast(s):
    cur = s
    vals = []
    for k in range(PEER_TOPK):
        m = jnp.max(cur, axis=0, keepdims=True)
        cur = jnp.where(cur == m, -MARK * (PEER_TOPK + k), cur)
        vals.append(m)
    marked = cur <= -MARK * PEER_TOPK
    rank = jnp.where(marked, cur * (-1.0 / MARK) - float(PEER_TOPK), 127.0)
    count = jnp.sum(jnp.where(marked, 1.0, 0.0), axis=0, keepdims=True)
    return rank, jnp.concatenate(vals, axis=0), count


def _cand_groups(v0, v1, combine, pad):
    sub = lax.broadcasted_iota(jnp.int32, (SUBLANES, v0.shape[1]), 0)
    first = combine(v0[0:1], v1)
    groups = [first[:SUBLANES], first[SUBLANES:]]
    for a in range(1, len(CAND_B)):
        piece = combine(v0[a:a + 1], v1[0:SUBLANES])
        groups.append(jnp.where(sub < CAND_B[a], piece, pad))
    groups.append(combine(v0[SUBLANES:], v1[0:1]))
    return groups


def _cand_list():
    out = [(b, 0, b) for b in range(CAND_B[0])]
    for a in range(1, len(CAND_B)):
        out += [(PEER_TOPK + SUBLANES * (a - 1) + b, a, b) for b in range(CAND_B[a])]
    base = PEER_TOPK + SUBLANES * (len(CAND_B) - 1)
    out += [(base + a - SUBLANES, a, 0) for a in range(SUBLANES, PEER_TOPK)]
    return out


def _select16(v0, v1):
    groups = _cand_groups(v0, v1, lambda x, y: x + y, -jnp.inf)
    sub = lax.broadcasted_iota(jnp.int32, groups[0].shape, 0)
    beaten = [jnp.zeros(groups[0].shape, F32) for _ in groups]
    for row, a, b in _cand_list():
        vc = v0[a:a + 1] + v1[b:b + 1]
        gc, rc = divmod(row, SUBLANES)
        for gi, grp in enumerate(groups):
            if gi == gc:
                first = jnp.where(sub > rc, jnp.where(vc >= grp, 1.0, 0.0), jnp.where(vc > grp, 1.0, 0.0))
                beaten[gi] = beaten[gi] + first
            else:
                first = (vc > grp) if gi < gc else (vc >= grp)
                beaten[gi] = jnp.where(first, beaten[gi] + 1.0, beaten[gi])
    return [jnp.where(bt < float(PEER_TOPK), 1.0, 0.0) for bt in beaten]


def _select16_fast(v0, v1):
    groups = _cand_groups(v0, v1, lambda x, y: x + y, -jnp.inf)
    cur = groups
    thr = None
    for _ in range(PEER_TOPK):
        m = cur[0]
        for grp in cur[1:]:
            m = jnp.maximum(m, grp)
        thr = jnp.max(m, axis=0, keepdims=True)
        cur = [jnp.where(grp == thr, -jnp.inf, grp) for grp in cur]
    sel = [jnp.where(grp >= thr, 1.0, 0.0) for grp in groups]
    total = sel[0]
    for sg in sel[1:]:
        total = total + sg
    return sel, jnp.sum(total, axis=0, keepdims=True)


def _route_head(st_ref, re_ref, nc_ref, lt, hd, exact):
    s0 = st_ref[lt, 2 * hd]
    s1 = st_ref[lt, 2 * hd + 1]
    worst = None
    if exact:
        (rank0, v0), (rank1, v1) = _top16_exact(s0), _top16_exact(s1)
        sel = _select16(v0, v1)
    else:
        rank0, v0, c0 = _top16_fast(s0)
        rank1, v1, c1 = _top16_fast(s1)
        sel, c2 = _select16_fast(v0, v1)
        worst = jnp.maximum(jnp.maximum(jnp.abs(c0 - float(PEER_TOPK)), jnp.abs(c1 - float(PEER_TOPK))),
                            jnp.abs(c2 - float(PEER_TOPK)))
    e0 = jnp.exp(v0 - v0[0:1])
    e1 = jnp.exp(v1 - v1[0:1])
    prod = _cand_groups(e0, e1, lambda x, y: x * y, 0.0)
    zsum = None
    for sg, pg in zip(sel, prod):
        part = jnp.sum(sg * pg, axis=0, keepdims=True)
        zsum = part if zsum is None else zsum + part
    cnt = jnp.zeros(rank0.shape, F32)
    for a in range(PEER_TOPK):
        if a == 0:
            n_a = jnp.sum(sel[0] + sel[1], axis=0, keepdims=True)
        elif a < SUBLANES:
            n_a = jnp.sum(sel[1 + a], axis=0, keepdims=True)
        else:
            n_a = sel[-1][a - SUBLANES:a - SUBLANES + 1]
        cnt = jnp.where(rank0 == float(a), n_a, cnt)
    coef = jnp.where(rank0 < float(PEER_TOPK), jnp.exp(s0 - v0[0:1]) / zsum, 0.0)
    e1f = jnp.where(rank1 < float(PEER_TOPK), jnp.exp(s1 - v1[0:1]), 0.0)
    re_ref[0, hd] = rank1.astype(re_ref.dtype)
    re_ref[1, hd] = e1f.astype(re_ref.dtype)
    nc_ref[0, hd, 0] = cnt
    nc_ref[1, hd, 0] = coef
    return worst


def _route_kernel(ht_ref, wq_ref, sk_ref, re_ref, nc_ref, st_ref, *, heads):
    lt = pl.program_id(1)
    kq = sk_ref.shape[2]

    @pl.when(lt == 0)
    def _():
        qt = jnp.dot(wq_ref[...], ht_ref[...], preferred_element_type=F32).astype(BF16)
        for hn in range(2 * heads):
            st = jnp.dot(sk_ref[hn % 2], qt[hn * kq:(hn + 1) * kq], preferred_element_type=F32)
            for ti in range(st_ref.shape[0]):
                st_ref[ti, hn] = st[:, ti * LANES:(ti + 1) * LANES]

    group = ROUTE_HEADS_PER_ITER

    def group_body(gi, carry):
        worst = None
        for k in range(group):
            dev = _route_head(st_ref, re_ref, nc_ref, lt, gi * group + k, exact=False)
            worst = dev if worst is None else jnp.maximum(worst, dev)

        @pl.when(jnp.max(worst) > 0.0)
        def _():
            for k in range(group):
                _route_head(st_ref, re_ref, nc_ref, lt, gi * group + k, exact=True)

        return carry

    lax.fori_loop(0, heads // group, group_body, 0)


def _route(h2t, wq_t, subkeys, heads, tm):
    d, t = h2t.shape
    keys = subkeys.shape[1]
    n_lt = tm // LANES
    assert keys == LANES
    return pl.pallas_call(
        functools.partial(_route_kernel, heads=heads),
        grid=(t // tm, n_lt),
        in_specs=[pl.BlockSpec((d, tm), lambda i, l: (0, i)),
                  pl.BlockSpec(wq_t.shape, lambda i, l: (0, 0)),
                  pl.BlockSpec(subkeys.shape, lambda i, l: (0, 0, 0))],
        out_specs=[pl.BlockSpec((2, heads, keys, LANES), lambda i, l: (0, 0, 0, i * n_lt + l)),
                   pl.BlockSpec((2, heads, 1, keys, LANES), lambda i, l: (0, 0, i * n_lt + l, 0, 0))],
        out_shape=[jax.ShapeDtypeStruct((2, heads, keys, t), BF16),
                   jax.ShapeDtypeStruct((2, heads, t // LANES, keys, LANES), F32)],
        scratch_shapes=[pltpu.VMEM((n_lt, 2 * heads, keys, LANES), F32)],
        compiler_params=_cparams(("parallel", "arbitrary")),
        name="peer_route",
    )(h2t, wq_t, subkeys)


def _peer_kernel(ht_ref, u_ref, vt_ref, re_ref, nc_ref, x1_ref, gate_ref, fg_ref, o_ref,
                 acc_ref, a_ref, w_ref, *, heads):
    j = pl.program_id(1)
    pack = 2 * SUBLANES

    @pl.when(j == 0)
    def _():
        acc_ref[...] = jnp.zeros_like(acc_ref)

    te, tm = a_ref.shape
    wc = PEER_COLS
    for r0 in range(0, te, PEER_SUB):
        rows = slice(r0, r0 + PEER_SUB)
        for c0 in range(0, tm, wc):
            cs = slice(c0, c0 + wc)
            tiles = range(c0 // LANES, (c0 + wc) // LANES)
            a_ref[rows, cs] = jnp.dot(u_ref[rows, :], ht_ref[:, cs], preferred_element_type=F32)
            for sl in range(r0 // LANES, (r0 + PEER_SUB) // LANES):
                rs = slice(sl * LANES, (sl + 1) * LANES)
                g = None
                for hd in range(heads):
                    nrow = jnp.concatenate([jnp.broadcast_to(nc_ref[0, hd, ti, sl:sl + 1, :], (pack, LANES))
                                            for ti in tiles], axis=1).astype(BF16)
                    crow = jnp.concatenate([jnp.broadcast_to(nc_ref[1, hd, ti, sl:sl + 1, :], (pack, LANES))
                                            for ti in tiles], axis=1).astype(BF16)
                    r1 = re_ref[0, hd, :, cs].reshape(LANES // pack, pack, wc)
                    e1 = re_ref[1, hd, :, cs].reshape(LANES // pack, pack, wc)
                    term = jnp.where(r1 < nrow[None], e1, jnp.zeros_like(e1)) * crow[None]
                    g = term if g is None else g + term
                w_ref[rs, cs] = _gelu(a_ref[rs, cs]).astype(BF16) * g.reshape(LANES, wc)
            acc_ref[:, cs] += jnp.dot(vt_ref[:, rows], w_ref[rows, cs], preferred_element_type=F32)

    @pl.when(j == pl.num_programs(1) - 1)
    def _():
        x2 = x1_ref[...] + gate_ref[0] * acc_ref[...].T
        o_ref[...] = _rms(x2, fg_ref[...])


def _peer(h2t, u_tab, v_t, route_re, route_nc, x1, gate2, final_g, n_per_batch, tm, te):
    d, t = h2t.shape
    e = u_tab.shape[0]
    heads, keys = route_re.shape[1], route_re.shape[2]
    assert keys == LANES and te % (SUBLANES * LANES) == 0 and n_per_batch % tm == 0
    return pl.pallas_call(
        functools.partial(_peer_kernel, heads=heads),
        grid=(t // tm, e // te),
        in_specs=[pl.BlockSpec((d, tm), lambda i, j: (0, i)),
                  pl.BlockSpec((te, d), lambda i, j: (j, 0)),
                  pl.BlockSpec((d, te), lambda i, j: (0, j)),
                  pl.BlockSpec((2, heads, keys, tm), lambda i, j: (0, 0, 0, i)),
                  pl.BlockSpec((2, heads, tm // LANES, te // LANES, LANES), lambda i, j: (0, 0, i, j, 0)),
                  pl.BlockSpec((tm, d), lambda i, j: (i, 0)),
                  pl.BlockSpec((1, 1, d), lambda i, j: ((i * tm) // n_per_batch, 0, 0)),
                  pl.BlockSpec((1, d), lambda i, j: (0, 0))],
        out_specs=pl.BlockSpec((tm, d), lambda i, j: (i, 0)),
        out_shape=jax.ShapeDtypeStruct((t, d), F32),
        scratch_shapes=[pltpu.VMEM((d, tm), F32),
                        pltpu.VMEM((te, tm), F32),
                        pltpu.VMEM((te, tm), BF16)],
        compiler_params=_cparams(("parallel", "arbitrary")),
        name="peer_dense",
    )(h2t, u_tab, v_t, route_re, route_nc, x1, gate2, final_g)


def _layer(x, ctx, c, c_ctx, p, final_g):
    b, n, d = x.shape
    n_ctx = ctx.shape[1]
    width = p['w_glu'].shape[0]
    col_k, col_v, col_q = width, 2 * width, 3 * width
    col_ga, col_gb = 4 * width, 4 * width + d
    heads_p = p['peer_w_q'].shape[1] // (2 * p['peer_subkeys'].shape[2])
    assert b % SUBLANES == 0 and n % (GRID_W * SSM_CHUNK) == 0 and n_ctx % SSM_CHUNK == 0
    assert width % LANES == 0 and col_ga % d == 0

    rows = 2 * SUBLANES
    cond = jnp.zeros((rows, d), F32).at[:b].set(c).at[b].set(c_ctx)
    mod = _adaln(cond, p['w_mod'], p['b_mod'])
    chunks = [mod[:, i * d:(i + 1) * d] for i in range(6)]
    shift1, scale1, gate1, shift2, scale2, gate2 = [m[:b, None, :] for m in chunks]
    shift_c, scale_c = chunks[0][b:b + 1, None, :], chunks[1][b:b + 1, None, :]

    w_in = p['w_in'].astype(BF16)
    g1 = p['norm1_g'].reshape(1, d)
    px = _inproj(x, g1, shift1, scale1, w_in, tm=256)
    pc = _inproj(ctx, g1, shift_c, scale_c, w_in[:, :col_q], tm=n_ctx)

    w_all, m_out, a_tab = _s5_tables(p['ssm_a_re'], p['ssm_a_im'], p['ssm_log_dt'], p['ssm_b_re'],
                                     p['ssm_b_im'], p['ssm_c_re'], p['ssm_c_im'], p['ssm_d'])
    cc, cl = n_ctx // SSM_CHUNK, n // SSM_CHUNK
    u_all = jnp.concatenate([_to_groups(pc[..., :width]), _to_groups(px[..., :width])], axis=1)
    groups = u_all.shape[0]
    u_all = u_all.reshape(groups, (cc + cl) * b, SSM_CHUNK * SSM_GROUP)
    y_g = _s5(u_all, w_all, m_out, a_tab, b, cc, cl)
    y_ssm = (y_g.reshape(groups, cl, b, SSM_CHUNK, SSM_GROUP)
             .transpose(2, 1, 3, 0, 4).reshape(b, n, width))

    o_na = _na(px, pc, _na_bias(p['na_rpb']), width, col_k, col_v, col_q)

    x1, h2t = _merge(x, y_ssm, o_na, px, gate1, shift2, scale2, p['norm2_g'].reshape(1, d),
                     p['w_glu'].astype(BF16), p['b_glu'].reshape(1, width),
                     p['w_branch_a'].astype(BF16), p['w_branch_b'].astype(BF16),
                     p['w_out'].astype(BF16), col_ga, col_gb, tm=256)

    t = b * n
    route_re, route_nc = _route(h2t, p['peer_w_q'].T.astype(BF16), p['peer_subkeys'].astype(BF16),
                                heads_p, tm=512)
    out = _peer(h2t, p['peer_u'].astype(BF16), p['peer_v'].T.astype(BF16), route_re, route_nc,
                x1.reshape(t, d), gate2, final_g.reshape(1, d), n, tm=512, te=2048)
    return out.reshape(b, n, d)


def kernel(x, c, ctx, c_ctx, w_mod, b_mod, norm1_g, norm2_g, w_in, ssm_a_re, ssm_a_im, ssm_log_dt,
           ssm_b_re, ssm_b_im, ssm_c_re, ssm_c_im, ssm_d, w_glu, b_glu, w_branch_a, w_branch_b, na_rpb,
           w_out, peer_w_q, peer_subkeys, peer_u, peer_v, final_g):
    assert w_mod.shape[0] == 1, "single-layer stack"
    p = {
        'w_mod': w_mod[0], 'b_mod': b_mod[0], 'norm1_g': norm1_g[0], 'norm2_g': norm2_g[0],
        'w_in': w_in[0],
        'ssm_a_re': ssm_a_re[0], 'ssm_a_im': ssm_a_im[0], 'ssm_log_dt': ssm_log_dt[0],
        'ssm_b_re': ssm_b_re[0], 'ssm_b_im': ssm_b_im[0], 'ssm_c_re': ssm_c_re[0], 'ssm_c_im': ssm_c_im[0],
        'ssm_d': ssm_d[0], 'w_glu': w_glu[0], 'b_glu': b_glu[0],
        'w_branch_a': w_branch_a[0], 'w_branch_b': w_branch_b[0], 'na_rpb': na_rpb[0], 'w_out': w_out[0],
        'peer_w_q': peer_w_q[0], 'peer_subkeys': peer_subkeys[0], 'peer_u': peer_u[0], 'peer_v': peer_v[0],
    }
    return _layer(x, ctx, c, c_ctx, p, final_g)
```

```python
import functools
import math

import jax
import jax.numpy as jnp
import numpy as np
from jax import lax
from jax.experimental import pallas as pl
from jax.experimental.pallas import tpu as pltpu

EPS = 1e-6
F32 = jnp.float32
BF16 = jnp.bfloat16

SSM_GROUP = 16
SSM_CHUNK = 16
NA_HEAD_DIM = 64
NA_ROWS_MAX = 8
NA_COLS = 16
NA_ROWS_PER_STEP = 2
GRID_W = 64
PEER_TOPK = 16
LANES = 128
SUBLANES = 8
NEG = -1e30
MARK = 2.0 ** 100
PEER_COLS = 256
PEER_SUB = 1024
ROUTE_HEADS_PER_ITER = 2

VMEM_LIMIT = 56 * 1024 * 1024


def _cparams(sem):
    return pltpu.CompilerParams(dimension_semantics=sem, vmem_limit_bytes=VMEM_LIMIT)


def _sigmoid(x):
    return 1.0 / (1.0 + jnp.exp(-x))


_GELU_K1 = -2.0 * math.sqrt(2.0 / math.pi) * math.log2(math.e)
_GELU_K2 = _GELU_K1 * 0.044715


def _gelu(x):
    t = x * (_GELU_K1 + _GELU_K2 * (x * x))
    return x / (1.0 + jnp.exp2(t))


def _rms(x, g):
    return x * lax.rsqrt(jnp.mean(x * x, axis=-1, keepdims=True) + EPS) * g


def _adaln_kernel(c_ref, w_ref, b_ref, o_ref):
    c = c_ref[...]
    a = c * _sigmoid(c)
    o_ref[...] = jnp.dot(a, w_ref[...], preferred_element_type=F32,
                         precision=lax.Precision.HIGHEST) + b_ref[...]


def _adaln(cond, w_mod, b_mod):
    rows, d = cond.shape
    cols = w_mod.shape[1]
    tn = 1536
    return pl.pallas_call(
        _adaln_kernel,
        grid=(cols // tn,),
        in_specs=[pl.BlockSpec((rows, d), lambda j: (0, 0)),
                  pl.BlockSpec((d, tn), lambda j: (0, j)),
                  pl.BlockSpec((1, tn), lambda j: (0, j))],
        out_specs=pl.BlockSpec((rows, tn), lambda j: (0, j)),
        out_shape=jax.ShapeDtypeStruct((rows, cols), F32),
        compiler_params=_cparams(("arbitrary",)),
        name="adaln",
    )(cond, w_mod, b_mod.reshape(1, cols))


def _inproj_kernel(x_ref, g_ref, shift_ref, scale_ref, w_ref, o_ref):
    h = _rms(x_ref[0], g_ref[...]) * (1.0 + scale_ref[0]) + shift_ref[0]
    o_ref[0] = jnp.dot(h.astype(BF16), w_ref[...], preferred_element_type=F32).astype(o_ref.dtype)


def _inproj(x, g, shift, scale, w, tm):
    b, n, d = x.shape
    cols = w.shape[1]
    per_batch = shift.shape[0] == b
    mod_map = (lambda i, j: (i, 0, 0)) if per_batch else (lambda i, j: (0, 0, 0))
    return pl.pallas_call(
        _inproj_kernel,
        grid=(b, n // tm),
        in_specs=[pl.BlockSpec((1, tm, d), lambda i, j: (i, j, 0)),
                  pl.BlockSpec((1, d), lambda i, j: (0, 0)),
                  pl.BlockSpec((1, 1, d), mod_map),
                  pl.BlockSpec((1, 1, d), mod_map),
                  pl.BlockSpec((d, cols), lambda i, j: (0, 0))],
        out_specs=pl.BlockSpec((1, tm, cols), lambda i, j: (i, j, 0)),
        out_shape=jax.ShapeDtypeStruct((b, n, cols), BF16),
        compiler_params=_cparams(("parallel", "parallel")),
        name="inproj",
    )(x, g, shift, scale, w)


def _cmul(ar, ai, br, bi):
    return ar * br - ai * bi, ar * bi + ai * br


def _s5_tables(a_re, a_im, log_dt, b_re, b_im, c_re, c_im, d_skip):
    L, H = SSM_CHUNK, SSM_GROUP
    hp = lax.Precision.HIGHEST
    ar, ai = a_re.astype(F32), a_im.astype(F32)
    dt = jnp.exp(log_dt.astype(F32))[..., None]
    xr, xi = ar * dt, ai * dt
    k = jnp.arange(L + 1, dtype=F32)[:, None, None, None]
    mag = jnp.exp(k * xr[None])
    pwr, pwi = mag * jnp.cos(k * xi[None]), mag * jnp.sin(k * xi[None])
    den = ar * ar + ai * ai
    ur, ui = pwr[1] - 1.0, pwi[1]
    qr, qi = (ur * ar + ui * ai) / den, (ui * ar - ur * ai) / den
    bbr, bbi = _cmul(qr[..., None], qi[..., None], b_re.astype(F32), b_im.astype(F32))
    cr, ci = c_re.astype(F32), c_im.astype(F32)
    g, p = ar.shape[1], ar.shape[2]

    wr, wi = _cmul(cr[None], ci[None], pwr[:L, :, :, None, :], pwi[:L, :, :, None, :])
    kern = (jnp.einsum('tdghp,dgpk->tdghk', wr, bbr, precision=hp)
            - jnp.einsum('tdghp,dgpk->tdghk', wi, bbi, precision=hp))
    t_idx = np.arange(L)
    lag_f = t_idx[None, :] - t_idx[:, None]
    sel_f = (lag_f[None] == t_idx[:, None, None]).astype(np.float32)
    sel_r = (-lag_f[None] == t_idx[:, None, None]).astype(np.float32)
    m_f = jnp.einsum('sjt,sghk->gjkth', sel_f, kern[:, 0], precision=hp)
    m_r = jnp.einsum('sjt,sghk->gjkth', sel_r, kern[:, 1], precision=hp)
    skip = d_skip.astype(F32).reshape(g, H)
    eye_l, eye_h = np.eye(L, dtype=np.float32), np.eye(H, dtype=np.float32)
    m_f = m_f + eye_l[None, :, None, :, None] * eye_h[None, None, :, None, :] * skip[:, None, None, None, :]
    m_f = m_f.reshape(g, L * H, L * H)
    m_r = m_r.reshape(g, L * H, L * H)

    def m_in(d, er, ei):
        zr, zi = _cmul(er[..., None], ei[..., None], bbr[d][None], bbi[d][None])
        zr = zr.transpose(1, 0, 3, 2).reshape(g, L * H, p)
        zi = zi.transpose(1, 0, 3, 2).reshape(g, L * H, p)
        return jnp.concatenate([zr, zi], -1), jnp.concatenate([zi, zr], -1)

    in_f, insw_f = m_in(0, pwr[:L, 0][::-1], pwi[:L, 0][::-1])
    in_r, insw_r = m_in(1, pwr[:L, 1], pwi[:L, 1])

    def m_out(d, er, ei):
        vr, vi = _cmul(cr[d][None], ci[d][None], er[:, :, None, :], ei[:, :, None, :])
        vr = vr.transpose(1, 3, 0, 2).reshape(g, p, L * H)
        vi = vi.transpose(1, 3, 0, 2).reshape(g, p, L * H)
        return jnp.concatenate([vr, -vi], 1)

    out_f = m_out(0, pwr[1:, 0], pwi[1:, 0])
    out_r = m_out(1, pwr[1:, 1][::-1], pwi[1:, 1][::-1])

    def a_rows(d):
        alr, ali = pwr[L, d], pwi[L, d]
        return [jnp.concatenate([alr, alr], -1),
                jnp.concatenate([-ali, ali], -1),
                jnp.concatenate([ali, -ali], -1)]

    rows = a_rows(0) + a_rows(1)
    rows = rows + [jnp.zeros_like(rows[0])] * 2
    a_tab = jnp.stack(rows, 1)
    w_all = jnp.concatenate([m_f, m_r, in_f, insw_f, in_r, insw_r], -1).astype(BF16)
    return w_all, jnp.stack([out_f, out_r], 1).astype(BF16), a_tab


def _s5_kernel(u_ref, w_ref, mo_ref, a_ref, y_ref, z_ref, sf_ref, sr_ref, *, nb, n_ctx, n_lat):
    lh = SSM_CHUNK * SSM_GROUP
    u = u_ref[0]
    z_ref[...] = jnp.dot(u, w_ref[0, :, 2 * lh:], preferred_element_type=F32)
    a = a_ref[0]
    n_all = n_ctx + n_lat

    def run(col, a1, a2, a2sw, order, s_ref, steps, carry):
        def body(kk, c):
            s, ssw = c
            r0 = pl.multiple_of(order(kk) * nb, nb)
            s_ref[pl.ds(r0, nb), :] = s
            x = z_ref[pl.ds(r0, nb), col:col + LANES]
            xsw = z_ref[pl.ds(r0, nb), col + LANES:col + 2 * LANES]
            return a1 * s + a2 * ssw + x, a1 * ssw + a2sw * s + xsw
        return lax.fori_loop(0, steps, body, carry, unroll=8)

    zero = jnp.zeros((nb, LANES), F32)
    run(0, a[0:1], a[1:2], a[2:3], lambda kk: kk, sf_ref, n_all, (zero, zero))
    c = run(2 * LANES, a[3:4], a[4:5], a[5:6], lambda kk: n_ctx - 1 - kk, sr_ref, n_ctx, (zero, zero))
    run(2 * LANES, a[3:4], a[4:5], a[5:6], lambda kk: n_all - 1 - kk, sr_ref, n_lat, c)

    lat0 = n_ctx * nb
    ul = u_ref[0, lat0:, :]
    yi = jnp.dot(ul, w_ref[0, :, :2 * lh], preferred_element_type=F32)
    y = yi[:, :lh] + yi[:, lh:]
    y = y + jnp.dot(sf_ref[lat0:, :].astype(BF16), mo_ref[0, 0], preferred_element_type=F32)
    y = y + jnp.dot(sr_ref[lat0:, :].astype(BF16), mo_ref[0, 1], preferred_element_type=F32)
    y_ref[0] = y.astype(y_ref.dtype)


def _s5(u_all, w_all, m_out, a_tab, nb, n_ctx, n_lat):
    g, r, lh = u_all.shape
    rl = n_lat * nb
    return pl.pallas_call(
        functools.partial(_s5_kernel, nb=nb, n_ctx=n_ctx, n_lat=n_lat),
        grid=(g,),
        in_specs=[pl.BlockSpec((1, r, lh), lambda i: (i, 0, 0)),
                  pl.BlockSpec((1, lh, 4 * lh), lambda i: (i, 0, 0)),
                  pl.BlockSpec((1, 2, LANES, lh), lambda i: (i, 0, 0, 0)),
                  pl.BlockSpec((1, 8, LANES), lambda i: (i, 0, 0))],
        out_specs=pl.BlockSpec((1, rl, lh), lambda i: (i, 0, 0)),
        out_shape=jax.ShapeDtypeStruct((g, rl, lh), BF16),
        scratch_shapes=[pltpu.VMEM((r, 4 * LANES), F32),
                        pltpu.VMEM((r, LANES), F32),
                        pltpu.VMEM((r, LANES), F32)],
        compiler_params=_cparams(("parallel",)),
        name="s5_scan",
    )(u_all, w_all, m_out, a_tab)


def _to_groups(u):
    b, n, w = u.shape
    g = w // SSM_GROUP
    return (u.reshape(b, n // SSM_CHUNK, SSM_CHUNK, g, SSM_GROUP)
            .transpose(3, 1, 0, 2, 4).reshape(g, n // SSM_CHUNK, b, SSM_CHUNK * SSM_GROUP))


def _na_bias(rpb):
    kh = NA_ROWS_MAX
    cols = np.arange(GRID_W)
    c0 = np.clip(cols - NA_COLS // 2, 0, GRID_W - NA_COLS)
    col_in = (cols[None, :] >= c0[:, None]) & (cols[None, :] < c0[:, None] + NA_COLS)
    dc = np.clip(cols[None, :] - cols[:, None] + NA_COLS - 1, 0, 2 * NA_COLS - 2)
    dr = np.arange(kh)[None, :] - np.arange(kh)[:, None] + NA_ROWS_MAX - 1
    pick_r = (dr[:, :, None] == np.arange(2 * NA_ROWS_MAX - 1)).astype(np.float32)
    pick_c = (dc[:, :, None] == np.arange(2 * NA_COLS - 1)).astype(np.float32)
    hp = lax.Precision.HIGHEST
    rows = jnp.einsum('hrc,vnr->vhnc', rpb.astype(F32), pick_r, precision=hp)
    bias = jnp.einsum('vhnc,qkc->vhqnk', rows, pick_c, precision=hp)
    bias = jnp.where(col_in[None, None, :, None, :], bias, NEG)
    heads = rpb.shape[0]
    return bias.reshape(kh, heads, GRID_W, kh * GRID_W)


def _na_kernel(q_ref, k_ref, v_ref, kc_ref, vc_ref, bias_ref, o_ref, *, rows, kh):
    nband = kh * GRID_W
    lane = lax.broadcasted_iota(jnp.int32, (1, LANES), 1)
    lo = lane < NA_HEAD_DIM
    scale = NA_HEAD_DIM ** -0.5
    nt = (((1,), (1,)), ((), ()))
    n_pairs = q_ref.shape[2] // LANES
    for rr in range(NA_ROWS_PER_STEP):
        r = pl.program_id(1) * NA_ROWS_PER_STEP + rr
        r0 = jnp.clip(r - kh // 2, 0, rows - kh)
        start = pl.multiple_of(r0 * GRID_W, GRID_W)
        var = r - r0
        qs = slice(rr * GRID_W, (rr + 1) * GRID_W)
        for hp in range(n_pairs):
            cs = slice(hp * LANES, (hp + 1) * LANES)
            q2 = q_ref[0, qs, cs]
            k2 = k_ref[0, pl.ds(start, nband), cs]
            v2 = v_ref[0, pl.ds(start, nband), cs]
            kc2 = kc_ref[0, :, cs]
            vc2 = vc_ref[0, :, cs]
            zq = jnp.zeros_like(q2)
            qm = jnp.concatenate([jnp.where(lo, q2, zq), jnp.where(lo, zq, q2)], axis=0)
            bias = jnp.concatenate([bias_ref[var, 2 * hp], bias_ref[var, 2 * hp + 1]], axis=0)
            s = lax.dot_general(qm, k2, nt, preferred_element_type=F32) * scale + bias
            sc = lax.dot_general(qm, kc2, nt, preferred_element_type=F32) * scale
            m = jnp.maximum(jnp.max(s, axis=-1, keepdims=True), jnp.max(sc, axis=-1, keepdims=True))
            p = jnp.exp(s - m)
            pc = jnp.exp(sc - m)
            den = jnp.sum(p, axis=-1, keepdims=True) + jnp.sum(pc, axis=-1, keepdims=True)
            o = (jnp.dot(p.astype(BF16), v2, preferred_element_type=F32)
                 + jnp.dot(pc.astype(BF16), vc2, preferred_element_type=F32)) / den
            o_ref[0, qs, cs] = jnp.where(lo, o[:GRID_W], o[GRID_W:]).astype(o_ref.dtype)


def _na(px, pc, bias, width, col_k, col_v, col_q):
    b, n, _ = px.shape
    n_ctx = pc.shape[1]
    rows = n // GRID_W
    kh = min(NA_ROWS_MAX, rows)
    assert kh == NA_ROWS_MAX, "sequence must span at least NA_ROWS_MAX grid rows"
    assert rows % NA_ROWS_PER_STEP == 0
    tq = NA_ROWS_PER_STEP * GRID_W
    return pl.pallas_call(
        functools.partial(_na_kernel, rows=rows, kh=kh),
        grid=(b, rows // NA_ROWS_PER_STEP),
        in_specs=[pl.BlockSpec((1, tq, width), lambda i, r: (i, r, col_q // width)),
                  pl.BlockSpec((1, n, width), lambda i, r: (i, 0, col_k // width)),
                  pl.BlockSpec((1, n, width), lambda i, r: (i, 0, col_v // width)),
                  pl.BlockSpec((1, n_ctx, width), lambda i, r: (i, 0, col_k // width)),
                  pl.BlockSpec((1, n_ctx, width), lambda i, r: (i, 0, col_v // width)),
                  pl.BlockSpec(bias.shape, lambda i, r: (0, 0, 0, 0))],
        out_specs=pl.BlockSpec((1, tq, width), lambda i, r: (i, r, 0)),
        out_shape=jax.ShapeDtypeStruct((b, n, width), BF16),
        compiler_params=_cparams(("parallel", "arbitrary")),
        name="na_mixer",
    )(px, px, px, pc, pc, bias)


def _merge_kernel(x_ref, y_ref, o_ref, ga_ref, gb_ref, gate_ref, shift_ref, scale_ref, g2_ref,
                  wglu_ref, bglu_ref, wa_ref, wb_ref, wo_ref, x1_ref, h2t_ref):
    z = _gelu(y_ref[0].astype(F32))
    gl = jnp.dot(z.astype(BF16), wglu_ref[...], preferred_element_type=F32) + bglu_ref[...]
    zz = z * _sigmoid(gl)
    ba = jnp.dot(zz.astype(BF16), wa_ref[...], preferred_element_type=F32)
    bb = jnp.dot(o_ref[0], wb_ref[...], preferred_element_type=F32)
    merged = _sigmoid(ga_ref[0].astype(F32)) * ba + _sigmoid(gb_ref[0].astype(F32)) * bb
    x1 = x_ref[0] + gate_ref[0] * jnp.dot(merged.astype(BF16), wo_ref[...], preferred_element_type=F32)
    x1_ref[0] = x1
    h2 = _rms(x1, g2_ref[...]) * (1.0 + scale_ref[0]) + shift_ref[0]
    h2t_ref[...] = h2.T.astype(h2t_ref.dtype)


def _merge(x, y_ssm, o_na, px, gate1, shift2, scale2, g2, w_glu, b_glu, w_a, w_b, w_o, col_ga, col_gb, tm):
    b, n, d = x.shape
    w = y_ssm.shape[2]
    tok = lambda i, j: (i, j, 0)
    per_b = lambda i, j: (i, 0, 0)
    full = lambda i, j: (0, 0)
    return pl.pallas_call(
        _merge_kernel,
        grid=(b, n // tm),
        in_specs=[pl.BlockSpec((1, tm, d), tok),
                  pl.BlockSpec((1, tm, w), tok),
                  pl.BlockSpec((1, tm, w), tok),
                  pl.BlockSpec((1, tm, d), lambda i, j: (i, j, col_ga // d)),
                  pl.BlockSpec((1, tm, d), lambda i, j: (i, j, col_gb // d)),
                  pl.BlockSpec((1, 1, d), per_b),
                  pl.BlockSpec((1, 1, d), per_b),
                  pl.BlockSpec((1, 1, d), per_b),
                  pl.BlockSpec((1, d), full),
                  pl.BlockSpec((w, w), full),
                  pl.BlockSpec((1, w), full),
                  pl.BlockSpec((w, d), full),
                  pl.BlockSpec((w, d), full),
                  pl.BlockSpec((d, d), full)],
        out_specs=[pl.BlockSpec((1, tm, d), tok),
                   pl.BlockSpec((d, tm), lambda i, j: (0, i * (n // tm) + j))],
        out_shape=[jax.ShapeDtypeStruct((b, n, d), F32), jax.ShapeDtypeStruct((d, b * n), BF16)],
        compiler_params=_cparams(("parallel", "parallel")),
        name="merge",
    )(x, y_ssm, o_na, px, px, gate1, shift2, scale2, g2, w_glu, b_glu, w_a, w_b, w_o)


CAND_B = (16, 8, 5, 4, 3, 2, 2, 2)


def _top16_exact(s):
    n = s.shape[0]
    iota = lax.broadcasted_iota(jnp.int32, s.shape, 0).astype(F32)
    cur = s
    rank = jnp.full(s.shape, 127.0, F32)
    vals = []
    for k in range(PEER_TOPK):
        m = jnp.max(cur, axis=0, keepdims=True)
        idx = jnp.min(jnp.where(cur == m, iota, float(n)), axis=0, keepdims=True)
        hit = iota == idx
        rank = jnp.where(hit, float(k), rank)
        cur = jnp.where(hit, -jnp.inf, cur)
        vals.append(m)
    return rank, jnp.concatenate(vals, axis=0)


def _top16_fast(s):
    cur = s
    vals = []
    for k in range(PEER_TOPK):
        m = jnp.max(cur, axis=0, keepdims=True)
        cur = jnp.where(cur == m, -MARK * (PEER_TOPK + k), cur)
        vals.append(m)
    marked = cur <= -MARK * PEER_TOPK
    rank = jnp.where(marked, cur * (-1.0 / MARK) - float(PEER_TOPK), 127.0)
    count = jnp.sum(jnp.where(marked, 1.0, 0.0), axis=0, keepdims=True)
    return rank, jnp.concatenate(vals, axis=0), count


def _cand_groups(v0, v1, combine, pad):
    sub = lax.broadcasted_iota(jnp.int32, (SUBLANES, v0.shape[1]), 0)
    first = combine(v0[0:1], v1)
    groups = [first[:SUBLANES], first[SUBLANES:]]
    for a in range(1, len(CAND_B)):
        piece = combine(v0[a:a + 1], v1[0:SUBLANES])
        groups.append(jnp.where(sub < CAND_B[a], piece, pad))
    groups.append(combine(v0[SUBLANES:], v1[0:1]))
    return groups


def _cand_list():
    out = [(b, 0, b) for b in range(CAND_B[0])]
    for a in range(1, len(CAND_B)):
        out += [(PEER_TOPK + SUBLANES * (a - 1) + b, a, b) for b in range(CAND_B[a])]
    base = PEER_TOPK + SUBLANES * (len(CAND_B) - 1)
    out += [(base + a - SUBLANES, a, 0) for a in range(SUBLANES, PEER_TOPK)]
    return out


def _select16(v0, v1):
    groups = _cand_groups(v0, v1, lambda x, y: x + y, -jnp.inf)
    sub = lax.broadcasted_iota(jnp.int32, groups[0].shape, 0)
    beaten = [jnp.zeros(groups[0].shape, F32) for _ in groups]
    for row, a, b in _cand_list():
        vc = v0[a:a + 1] + v1[b:b + 1]
        gc, rc = divmod(row, SUBLANES)
        for gi, grp in enumerate(groups):
            if gi == gc:
                first = jnp.where(sub > rc, jnp.where(vc >= grp, 1.0, 0.0), jnp.where(vc > grp, 1.0, 0.0))
                beaten[gi] = beaten[gi] + first
            else:
                first = (vc > grp) if gi < gc else (vc >= grp)
                beaten[gi] = jnp.where(first, beaten[gi] + 1.0, beaten[gi])
    return [jnp.where(bt < float(PEER_TOPK), 1.0, 0.0) for bt in beaten]


def _select16_fast(v0, v1):
    groups = _cand_groups(v0, v1, lambda x, y: x + y, -jnp.inf)
    cur = groups
    thr = None
    for _ in range(PEER_TOPK):
        m = cur[0]
        for grp in cur[1:]:
            m = jnp.maximum(m, grp)
        thr = jnp.max(m, axis=0, keepdims=True)
        cur = [jnp.where(grp == thr, -jnp.inf, grp) for grp in cur]
    sel = [jnp.where(grp >= thr, 1.0, 0.0) for grp in groups]
    total = sel[0]
    for sg in sel[1:]:
        total = total + sg
    return sel, jnp.sum(total, axis=0, keepdims=True)


def _route_head(st_ref, re_ref, nc_ref, lt, hd, exact):
    s0 = st_ref[lt, 2 * hd]
    s1 = st_ref[lt, 2 * hd + 1]
    worst = None
    if exact:
        (rank0, v0), (rank1, v1) = _top16_exact(s0), _top16_exact(s1)
        sel = _select16(v0, v1)
    else:
        rank0, v0, c0 = _top16_fast(s0)
        rank1, v1, c1 = _top16_fast(s1)
        sel, c2 = _select16_fast(v0, v1)
        worst = jnp.maximum(jnp.maximum(jnp.abs(c0 - float(PEER_TOPK)), jnp.abs(c1 - float(PEER_TOPK))),
                            jnp.abs(c2 - float(PEER_TOPK)))
    e0 = jnp.exp(v0 - v0[0:1])
    e1 = jnp.exp(v1 - v1[0:1])
    prod = _cand_groups(e0, e1, lambda x, y: x * y, 0.0)
    zsum = None
    for sg, pg in zip(sel, prod):
        part = jnp.sum(sg * pg, axis=0, keepdims=True)
        zsum = part if zsum is None else zsum + part
    cnt = jnp.zeros(rank0.shape, F32)
    for a in range(PEER_TOPK):
        if a == 0:
            n_a = jnp.sum(sel[0] + sel[1], axis=0, keepdims=True)
        elif a < SUBLANES:
            n_a = jnp.sum(sel[1 + a], axis=0, keepdims=True)
        else:
            n_a = sel[-1][a - SUBLANES:a - SUBLANES + 1]
        cnt = jnp.where(rank0 == float(a), n_a, cnt)
    coef = jnp.where(rank0 < float(PEER_TOPK), jnp.exp(s0 - v0[0:1]) / zsum, 0.0)
    e1f = jnp.where(rank1 < float(PEER_TOPK), jnp.exp(s1 - v1[0:1]), 0.0)
    re_ref[0, hd] = rank1.astype(re_ref.dtype)
    re_ref[1, hd] = e1f.astype(re_ref.dtype)
    nc_ref[0, hd, 0] = cnt
    nc_ref[1, hd, 0] = coef
    return worst


def _route_kernel(ht_ref, wq_ref, sk_ref, re_ref, nc_ref, st_ref, *, heads):
    lt = pl.program_id(1)
    kq = sk_ref.shape[2]

    @pl.when(lt == 0)
    def _():
        qt = jnp.dot(wq_ref[...], ht_ref[...], preferred_element_type=F32).astype(BF16)
        for hn in range(2 * heads):
            st = jnp.dot(sk_ref[hn % 2], qt[hn * kq:(hn + 1) * kq], preferred_element_type=F32)
            for ti in range(st_ref.shape[0]):
                st_ref[ti, hn] = st[:, ti * LANES:(ti + 1) * LANES]

    group = ROUTE_HEADS_PER_ITER

    def group_body(gi, carry):
        worst = None
        for k in range(group):
            dev = _route_head(st_ref, re_ref, nc_ref, lt, gi * group + k, exact=False)
            worst = dev if worst is None else jnp.maximum(worst, dev)

        @pl.when(jnp.max(worst) > 0.0)
        def _():
            for k in range(group):
                _route_head(st_ref, re_ref, nc_ref, lt, gi * group + k, exact=True)

        return carry

    lax.fori_loop(0, heads // group, group_body, 0)


def _route(h2t, wq_t, subkeys, heads, tm):
    d, t = h2t.shape
    keys = subkeys.shape[1]
    n_lt = tm // LANES
    assert keys == LANES
    return pl.pallas_call(
        functools.partial(_route_kernel, heads=heads),
        grid=(t // tm, n_lt),
        in_specs=[pl.BlockSpec((d, tm), lambda i, l: (0, i)),
                  pl.BlockSpec(wq_t.shape, lambda i, l: (0, 0)),
                  pl.BlockSpec(subkeys.shape, lambda i, l: (0, 0, 0))],
        out_specs=[pl.BlockSpec((2, heads, keys, LANES), lambda i, l: (0, 0, 0, i * n_lt + l)),
                   pl.BlockSpec((2, heads, 1, keys, LANES), lambda i, l: (0, 0, i * n_lt + l, 0, 0))],
        out_shape=[jax.ShapeDtypeStruct((2, heads, keys, t), BF16),
                   jax.ShapeDtypeStruct((2, heads, t // LANES, keys, LANES), F32)],
        scratch_shapes=[pltpu.VMEM((n_lt, 2 * heads, keys, LANES), F32)],
        compiler_params=_cparams(("parallel", "arbitrary")),
        name="peer_route",
    )(h2t, wq_t, subkeys)


def _peer_kernel(ht_ref, u_ref, vt_ref, re_ref, nc_ref, x1_ref, gate_ref, fg_ref, o_ref,
                 acc_ref, a_ref, w_ref, *, heads):
    j = pl.program_id(1)
    pack = 2 * SUBLANES

    @pl.when(j == 0)
    def _():
        acc_ref[...] = jnp.zeros_like(acc_ref)

    te, tm = a_ref.shape
    wc = PEER_COLS
    for r0 in range(0, te, PEER_SUB):
        rows = slice(r0, r0 + PEER_SUB)
        a_ref[rows, :] = jnp.dot(u_ref[rows, :], ht_ref[...], preferred_element_type=F32)
        for sl in range(r0 // LANES, (r0 + PEER_SUB) // LANES):
            rs = slice(sl * LANES, (sl + 1) * LANES)
            for c0 in range(0, tm, wc):
                cs = slice(c0, c0 + wc)
                tiles = range(c0 // LANES, (c0 + wc) // LANES)
                g = None
                for hd in range(heads):
                    nrow = jnp.concatenate([jnp.broadcast_to(nc_ref[0, hd, ti, sl:sl + 1, :], (pack, LANES))
                                            for ti in tiles], axis=1).astype(BF16)
                    crow = jnp.concatenate([jnp.broadcast_to(nc_ref[1, hd, ti, sl:sl + 1, :], (pack, LANES))
                                            for ti in tiles], axis=1).astype(BF16)
                    r1 = re_ref[0, hd, :, cs].reshape(LANES // pack, pack, wc)
                    e1 = re_ref[1, hd, :, cs].reshape(LANES // pack, pack, wc)
                    term = jnp.where(r1 < nrow[None], e1, jnp.zeros_like(e1)) * crow[None]
                    g = term if g is None else g + term
                w_ref[rs, cs] = _gelu(a_ref[rs, cs].astype(BF16)) * g.reshape(LANES, wc)
        acc_ref[...] += jnp.dot(vt_ref[:, rows], w_ref[rows, :], preferred_element_type=F32)

    @pl.when(j == pl.num_programs(1) - 1)
    def _():
        x2 = x1_ref[...] + gate_ref[0] * acc_ref[...].T
        o_ref[...] = _rms(x2, fg_ref[...])


def _peer(h2t, u_tab, v_t, route_re, route_nc, x1, gate2, final_g, n_per_batch, tm, te):
    d, t = h2t.shape
    e = u_tab.shape[0]
    heads, keys = route_re.shape[1], route_re.shape[2]
    assert keys == LANES and te % (SUBLANES * LANES) == 0 and n_per_batch % tm == 0
    return pl.pallas_call(
        functools.partial(_peer_kernel, heads=heads),
        grid=(t // tm, e // te),
        in_specs=[pl.BlockSpec((d, tm), lambda i, j: (0, i)),
                  pl.BlockSpec((te, d), lambda i, j: (j, 0)),
                  pl.BlockSpec((d, te), lambda i, j: (0, j)),
                  pl.BlockSpec((2, heads, keys, tm), lambda i, j: (0, 0, 0, i)),
                  pl.BlockSpec((2, heads, tm // LANES, te // LANES, LANES), lambda i, j: (0, 0, i, j, 0)),
                  pl.BlockSpec((tm, d), lambda i, j: (i, 0)),
                  pl.BlockSpec((1, 1, d), lambda i, j: ((i * tm) // n_per_batch, 0, 0)),
                  pl.BlockSpec((1, d), lambda i, j: (0, 0))],
        out_specs=pl.BlockSpec((tm, d), lambda i, j: (i, 0)),
        out_shape=jax.ShapeDtypeStruct((t, d), F32),
        scratch_shapes=[pltpu.VMEM((d, tm), F32),
                        pltpu.VMEM((te, tm), F32),
                        pltpu.VMEM((te, tm), BF16)],
        compiler_params=_cparams(("parallel", "arbitrary")),
        name="peer_dense",
    )(h2t, u_tab, v_t, route_re, route_nc, x1, gate2, final_g)


def _layer(x, ctx, c, c_ctx, p, final_g):
    b, n, d = x.shape
    n_ctx = ctx.shape[1]
    width = p['w_glu'].shape[0]
    col_k, col_v, col_q = width, 2 * width, 3 * width
    col_ga, col_gb = 4 * width, 4 * width + d
    heads_p = p['peer_w_q'].shape[1] // (2 * p['peer_subkeys'].shape[2])
    assert b % SUBLANES == 0 and n % (GRID_W * SSM_CHUNK) == 0 and n_ctx % SSM_CHUNK == 0
    assert width % LANES == 0 and col_ga % d == 0

    rows = 2 * SUBLANES
    cond = jnp.zeros((rows, d), F32).at[:b].set(c).at[b].set(c_ctx)
    mod = _adaln(cond, p['w_mod'], p['b_mod'])
    chunks = [mod[:, i * d:(i + 1) * d] for i in range(6)]
    shift1, scale1, gate1, shift2, scale2, gate2 = [m[:b, None, :] for m in chunks]
    shift_c, scale_c = chunks[0][b:b + 1, None, :], chunks[1][b:b + 1, None, :]

    w_in = p['w_in'].astype(BF16)
    g1 = p['norm1_g'].reshape(1, d)
    px = _inproj(x, g1, shift1, scale1, w_in, tm=256)
    pc = _inproj(ctx, g1, shift_c, scale_c, w_in[:, :col_q], tm=n_ctx)

    w_all, m_out, a_tab = _s5_tables(p['ssm_a_re'], p['ssm_a_im'], p['ssm_log_dt'], p['ssm_b_re'],
                                     p['ssm_b_im'], p['ssm_c_re'], p['ssm_c_im'], p['ssm_d'])
    cc, cl = n_ctx // SSM_CHUNK, n // SSM_CHUNK
    u_all = jnp.concatenate([_to_groups(pc[..., :width]), _to_groups(px[..., :width])], axis=1)
    groups = u_all.shape[0]
    u_all = u_all.reshape(groups, (cc + cl) * b, SSM_CHUNK * SSM_GROUP)
    y_g = _s5(u_all, w_all, m_out, a_tab, b, cc, cl)
    y_ssm = (y_g.reshape(groups, cl, b, SSM_CHUNK, SSM_GROUP)
             .transpose(2, 1, 3, 0, 4).reshape(b, n, width))

    o_na = _na(px, pc, _na_bias(p['na_rpb']), width, col_k, col_v, col_q)

    x1, h2t = _merge(x, y_ssm, o_na, px, gate1, shift2, scale2, p['norm2_g'].reshape(1, d),
                     p['w_glu'].astype(BF16), p['b_glu'].reshape(1, width),
                     p['w_branch_a'].astype(BF16), p['w_branch_b'].astype(BF16),
                     p['w_out'].astype(BF16), col_ga, col_gb, tm=256)

    t = b * n
    route_re, route_nc = _route(h2t, p['peer_w_q'].T.astype(BF16), p['peer_subkeys'].astype(BF16),
                                heads_p, tm=512)
    out = _peer(h2t, p['peer_u'].astype(BF16), p['peer_v'].T.astype(BF16), route_re, route_nc,
                x1.reshape(t, d), gate2, final_g.reshape(1, d), n, tm=512, te=2048)
    return out.reshape(b, n, d)


def kernel(x, c, ctx, c_ctx, w_mod, b_mod, norm1_g, norm2_g, w_in, ssm_a_re, ssm_a_im, ssm_log_dt,
           ssm_b_re, ssm_b_im, ssm_c_re, ssm_c_im, ssm_d, w_glu, b_glu, w_branch_a, w_branch_b, na_rpb,
           w_out, peer_w_q, peer_subkeys, peer_u, peer_v, final_g):
    assert w_mod.shape[0] == 1, "single-layer stack"
    p = {
        'w_mod': w_mod[0], 'b_mod': b_mod[0], 'norm1_g': norm1_g[0], 'norm2_g': norm2_g[0],
        'w_in': w_in[0],
        'ssm_a_re': ssm_a_re[0], 'ssm_a_im': ssm_a_im[0], 'ssm_log_dt': ssm_log_dt[0],
        'ssm_b_re': ssm_b_re[0], 'ssm_b_im': ssm_b_im[0], 'ssm_c_re': ssm_c_re[0], 'ssm_c_im': ssm_c_im[0],
        'ssm_d': ssm_d[0], 'w_glu': w_glu[0], 'b_glu': b_glu[0],
        'w_branch_a': w_branch_a[0], 'w_branch_b': w_branch_b[0], 'na_rpb': na_rpb[0], 'w_out': w_out[0],
        'peer_w_q': peer_w_q[0], 'peer_subkeys': peer_subkeys[0], 'peer_u': peer_u[0], 'peer_v': peer_v[0],
    }
    return _layer(x, ctx, c, c_ctx, p, final_g)
```

```python
import functools
import math

import jax
import jax.numpy as jnp
import numpy as np
from jax import lax
from jax.experimental import pallas as pl
from jax.experimental.pallas import tpu as pltpu

EPS = 1e-6
F32 = jnp.float32
BF16 = jnp.bfloat16

SSM_GROUP = 16
SSM_CHUNK = 16
NA_HEAD_DIM = 64
NA_ROWS_MAX = 8
NA_COLS = 16
NA_ROWS_PER_STEP = 4
GRID_W = 64
PEER_TOPK = 16
LANES = 128
SUBLANES = 8
NEG = -1e30
MARK = 2.0 ** 100
PEER_COLS = 256
PEER_SUB = 1024
ROUTE_HEADS_PER_ITER = 8

VMEM_LIMIT = 56 * 1024 * 1024


def _cparams(sem):
    return pltpu.CompilerParams(dimension_semantics=sem, vmem_limit_bytes=VMEM_LIMIT)


def _sigmoid(x):
    return 1.0 / (1.0 + jnp.exp(-x))


_GELU_K1 = -2.0 * math.sqrt(2.0 / math.pi) * math.log2(math.e)
_GELU_K2 = _GELU_K1 * 0.044715


def _gelu(x):
    t = x * (_GELU_K1 + _GELU_K2 * (x * x))
    return x / (1.0 + jnp.exp2(t))


def _rms(x, g):
    return x * lax.rsqrt(jnp.mean(x * x, axis=-1, keepdims=True) + EPS) * g


def _adaln_kernel(c_ref, w_ref, b_ref, o_ref):
    c = c_ref[...]
    a = c * _sigmoid(c)
    o_ref[...] = jnp.dot(a, w_ref[...], preferred_element_type=F32,
                         precision=lax.Precision.HIGHEST) + b_ref[...]


def _adaln(cond, w_mod, b_mod):
    rows, d = cond.shape
    cols = w_mod.shape[1]
    tn = 1536
    return pl.pallas_call(
        _adaln_kernel,
        grid=(cols // tn,),
        in_specs=[pl.BlockSpec((rows, d), lambda j: (0, 0)),
                  pl.BlockSpec((d, tn), lambda j: (0, j)),
                  pl.BlockSpec((1, tn), lambda j: (0, j))],
        out_specs=pl.BlockSpec((rows, tn), lambda j: (0, j)),
        out_shape=jax.ShapeDtypeStruct((rows, cols), F32),
        compiler_params=_cparams(("arbitrary",)),
        name="adaln",
    )(cond, w_mod, b_mod.reshape(1, cols))


def _inproj_kernel(x_ref, g_ref, shift_ref, scale_ref, w_ref, o_ref):
    h = _rms(x_ref[0], g_ref[...]) * (1.0 + scale_ref[0]) + shift_ref[0]
    o_ref[0] = jnp.dot(h.astype(BF16), w_ref[...], preferred_element_type=F32).astype(o_ref.dtype)


def _inproj(x, g, shift, scale, w, tm):
    b, n, d = x.shape
    cols = w.shape[1]
    per_batch = shift.shape[0] == b
    mod_map = (lambda i, j: (i, 0, 0)) if per_batch else (lambda i, j: (0, 0, 0))
    return pl.pallas_call(
        _inproj_kernel,
        grid=(b, n // tm),
        in_specs=[pl.BlockSpec((1, tm, d), lambda i, j: (i, j, 0)),
                  pl.BlockSpec((1, d), lambda i, j: (0, 0)),
                  pl.BlockSpec((1, 1, d), mod_map),
                  pl.BlockSpec((1, 1, d), mod_map),
                  pl.BlockSpec((d, cols), lambda i, j: (0, 0))],
        out_specs=pl.BlockSpec((1, tm, cols), lambda i, j: (i, j, 0)),
        out_shape=jax.ShapeDtypeStruct((b, n, cols), BF16),
        compiler_params=_cparams(("parallel", "parallel")),
        name="inproj",
    )(x, g, shift, scale, w)


def _cmul(ar, ai, br, bi):
    return ar * br - ai * bi, ar * bi + ai * br


def _s5_tables(a_re, a_im, log_dt, b_re, b_im, c_re, c_im, d_skip):
    L, H = SSM_CHUNK, SSM_GROUP
    hp = lax.Precision.HIGHEST
    ar, ai = a_re.astype(F32), a_im.astype(F32)
    dt = jnp.exp(log_dt.astype(F32))[..., None]
    xr, xi = ar * dt, ai * dt
    k = jnp.arange(L + 1, dtype=F32)[:, None, None, None]
    mag = jnp.exp(k * xr[None])
    pwr, pwi = mag * jnp.cos(k * xi[None]), mag * jnp.sin(k * xi[None])
    den = ar * ar + ai * ai
    ur, ui = pwr[1] - 1.0, pwi[1]
    qr, qi = (ur * ar + ui * ai) / den, (ui * ar - ur * ai) / den
    bbr, bbi = _cmul(qr[..., None], qi[..., None], b_re.astype(F32), b_im.astype(F32))
    cr, ci = c_re.astype(F32), c_im.astype(F32)
    g, p = ar.shape[1], ar.shape[2]

    wr, wi = _cmul(cr[None], ci[None], pwr[:L, :, :, None, :], pwi[:L, :, :, None, :])
    kern = jnp.einsum('tdghq,dgqk->tdghk', jnp.concatenate([wr, -wi], -1), jnp.concatenate([bbr, bbi], 2),
                      precision=hp)
    t_idx = np.arange(L)
    lag_f = t_idx[None, :] - t_idx[:, None]
    sel_f = (lag_f[None] == t_idx[:, None, None]).astype(np.float32)
    sel_r = (-lag_f[None] == t_idx[:, None, None]).astype(np.float32)
    m_f = jnp.einsum('sjt,sghk->gjkth', sel_f, kern[:, 0], precision=hp)
    m_r = jnp.einsum('sjt,sghk->gjkth', sel_r, kern[:, 1], precision=hp)
    skip = d_skip.astype(F32).reshape(g, H)
    eye_l, eye_h = np.eye(L, dtype=np.float32), np.eye(H, dtype=np.float32)
    m_f = m_f + eye_l[None, :, None, :, None] * eye_h[None, None, :, None, :] * skip[:, None, None, None, :]
    m_f = m_f.reshape(g, L * H, L * H)
    m_r = m_r.reshape(g, L * H, L * H)

    def m_in(d, er, ei):
        zr, zi = _cmul(er[..., None], ei[..., None], bbr[d][None], bbi[d][None])
        zr = zr.transpose(1, 0, 3, 2).reshape(g, L * H, p)
        zi = zi.transpose(1, 0, 3, 2).reshape(g, L * H, p)
        return jnp.concatenate([zr, zi], -1), jnp.concatenate([zi, zr], -1)

    in_f, insw_f = m_in(0, pwr[:L, 0][::-1], pwi[:L, 0][::-1])
    in_r, insw_r = m_in(1, pwr[:L, 1], pwi[:L, 1])

    def m_out(d, er, ei):
        vr, vi = _cmul(cr[d][None], ci[d][None], er[:, :, None, :], ei[:, :, None, :])
        vr = vr.transpose(1, 3, 0, 2).reshape(g, p, L * H)
        vi = vi.transpose(1, 3, 0, 2).reshape(g, p, L * H)
        return jnp.concatenate([vr, -vi], 1)

    out_f = m_out(0, pwr[1:, 0], pwi[1:, 0])
    out_r = m_out(1, pwr[1:, 1][::-1], pwi[1:, 1][::-1])

    def a_rows(d):
        alr, ali = pwr[L, d], pwi[L, d]
        return [jnp.concatenate([alr, alr], -1),
                jnp.concatenate([-ali, ali], -1),
                jnp.concatenate([ali, -ali], -1)]

    rows = a_rows(0) + a_rows(1)
    rows = rows + [jnp.zeros_like(rows[0])] * 2
    a_tab = jnp.stack(rows, 1)
    w_all = jnp.concatenate([m_f, m_r, in_f, insw_f, in_r, insw_r], -1).astype(BF16)
    return w_all, jnp.stack([out_f, out_r], 1).astype(BF16), a_tab


def _s5_kernel(u_ref, w_ref, mo_ref, a_ref, y_ref, z_ref, sf_ref, sr_ref, *, nb, n_ctx, n_lat):
    lh = SSM_CHUNK * SSM_GROUP
    u = u_ref[0]
    z_ref[...] = jnp.dot(u, w_ref[0, :, 2 * lh:], preferred_element_type=F32)
    a = a_ref[0]
    n_all = n_ctx + n_lat

    def run(col, a1, a2, a2sw, order, s_ref, steps, carry):
        def body(kk, c):
            s, ssw = c
            r0 = pl.multiple_of(order(kk) * nb, nb)
            s_ref[pl.ds(r0, nb), :] = s
            x = z_ref[pl.ds(r0, nb), col:col + LANES]
            xsw = z_ref[pl.ds(r0, nb), col + LANES:col + 2 * LANES]
            return a1 * s + a2 * ssw + x, a1 * ssw + a2sw * s + xsw
        return lax.fori_loop(0, steps, body, carry, unroll=8)

    zero = jnp.zeros((nb, LANES), F32)
    run(0, a[0:1], a[1:2], a[2:3], lambda kk: kk, sf_ref, n_all, (zero, zero))
    c = run(2 * LANES, a[3:4], a[4:5], a[5:6], lambda kk: n_ctx - 1 - kk, sr_ref, n_ctx, (zero, zero))
    run(2 * LANES, a[3:4], a[4:5], a[5:6], lambda kk: n_all - 1 - kk, sr_ref, n_lat, c)

    lat0 = n_ctx * nb
    ul = u_ref[0, lat0:, :]
    yi = jnp.dot(ul, w_ref[0, :, :2 * lh], preferred_element_type=F32)
    y = yi[:, :lh] + yi[:, lh:]
    y = y + jnp.dot(sf_ref[lat0:, :].astype(BF16), mo_ref[0, 0], preferred_element_type=F32)
    y = y + jnp.dot(sr_ref[lat0:, :].astype(BF16), mo_ref[0, 1], preferred_element_type=F32)
    y_ref[0] = y.astype(y_ref.dtype)


def _s5(u_all, w_all, m_out, a_tab, nb, n_ctx, n_lat):
    g, r, lh = u_all.shape
    rl = n_lat * nb
    return pl.pallas_call(
        functools.partial(_s5_kernel, nb=nb, n_ctx=n_ctx, n_lat=n_lat),
        grid=(g,),
        in_specs=[pl.BlockSpec((1, r, lh), lambda i: (i, 0, 0)),
                  pl.BlockSpec((1, lh, 4 * lh), lambda i: (i, 0, 0)),
                  pl.BlockSpec((1, 2, LANES, lh), lambda i: (i, 0, 0, 0)),
                  pl.BlockSpec((1, 8, LANES), lambda i: (i, 0, 0))],
        out_specs=pl.BlockSpec((1, rl, lh), lambda i: (i, 0, 0)),
        out_shape=jax.ShapeDtypeStruct((g, rl, lh), BF16),
        scratch_shapes=[pltpu.VMEM((r, 4 * LANES), F32),
                        pltpu.VMEM((r, LANES), F32),
                        pltpu.VMEM((r, LANES), F32)],
        compiler_params=_cparams(("parallel",)),
        name="s5_scan",
    )(u_all, w_all, m_out, a_tab)


def _to_groups(u):
    b, n, w = u.shape
    g = w // SSM_GROUP
    return (u.reshape(b, n // SSM_CHUNK, SSM_CHUNK, g, SSM_GROUP)
            .transpose(3, 1, 0, 2, 4).reshape(g, n // SSM_CHUNK, b, SSM_CHUNK * SSM_GROUP))


def _na_bias(rpb):
    kh = NA_ROWS_MAX
    cols = np.arange(GRID_W)
    c0 = np.clip(cols - NA_COLS // 2, 0, GRID_W - NA_COLS)
    col_in = (cols[None, :] >= c0[:, None]) & (cols[None, :] < c0[:, None] + NA_COLS)
    dc = np.clip(cols[None, :] - cols[:, None] + NA_COLS - 1, 0, 2 * NA_COLS - 2)
    dr = np.arange(kh)[None, :] - np.arange(kh)[:, None] + NA_ROWS_MAX - 1
    pick_r = (dr[:, :, None] == np.arange(2 * NA_ROWS_MAX - 1)).astype(np.float32)
    pick_c = (dc[:, :, None] == np.arange(2 * NA_COLS - 1)).astype(np.float32)
    hp = lax.Precision.HIGHEST
    rows = jnp.einsum('hrc,vnr->vhnc', rpb.astype(F32), pick_r, precision=hp)
    bias = jnp.einsum('vhnc,qkc->vhqnk', rows, pick_c, precision=hp)
    bias = jnp.where(col_in[None, None, :, None, :], bias, NEG)
    heads = rpb.shape[0]
    return bias.reshape(kh, heads, GRID_W, kh * GRID_W) * math.log2(math.e)


def _na_kernel(q_ref, k_ref, v_ref, kc_ref, vc_ref, bias_ref, o_ref, *, rows, kh):
    nband = kh * GRID_W
    lane = lax.broadcasted_iota(jnp.int32, (1, LANES), 1)
    lo = lane < NA_HEAD_DIM
    scale = NA_HEAD_DIM ** -0.5 * math.log2(math.e)
    nt = (((1,), (1,)), ((), ()))
    n_pairs = q_ref.shape[2] // LANES
    for rr in range(NA_ROWS_PER_STEP):
        r = pl.program_id(1) * NA_ROWS_PER_STEP + rr
        r0 = jnp.clip(r - kh // 2, 0, rows - kh)
        start = pl.multiple_of(r0 * GRID_W, GRID_W)
        var = r - r0
        qs = slice(rr * GRID_W, (rr + 1) * GRID_W)
        for hp in range(n_pairs):
            cs = slice(hp * LANES, (hp + 1) * LANES)
            q2 = q_ref[0, qs, cs]
            k2 = k_ref[0, pl.ds(start, nband), cs]
            v2 = v_ref[0, pl.ds(start, nband), cs]
            kc2 = kc_ref[0, :, cs]
            vc2 = vc_ref[0, :, cs]
            zq = jnp.zeros_like(q2)
            qm = jnp.concatenate([jnp.where(lo, q2, zq), jnp.where(lo, zq, q2)], axis=0)
            bias = jnp.concatenate([bias_ref[var, 2 * hp], bias_ref[var, 2 * hp + 1]], axis=0)
            s = lax.dot_general(qm, k2, nt, preferred_element_type=F32) * scale + bias
            sc = lax.dot_general(qm, kc2, nt, preferred_element_type=F32) * scale
            m = jnp.maximum(jnp.max(s, axis=-1, keepdims=True), jnp.max(sc, axis=-1, keepdims=True))
            p = jnp.exp2(s - m)
            pc = jnp.exp2(sc - m)
            den = jnp.sum(p, axis=-1, keepdims=True) + jnp.sum(pc, axis=-1, keepdims=True)
            o = (jnp.dot(p.astype(BF16), v2, preferred_element_type=F32)
                 + jnp.dot(pc.astype(BF16), vc2, preferred_element_type=F32)) / den
            o_ref[0, qs, cs] = jnp.where(lo, o[:GRID_W], o[GRID_W:]).astype(o_ref.dtype)


def _na(px, pc, bias, width, col_k, col_v, col_q):
    b, n, _ = px.shape
    n_ctx = pc.shape[1]
    rows = n // GRID_W
    kh = min(NA_ROWS_MAX, rows)
    assert kh == NA_ROWS_MAX, "sequence must span at least NA_ROWS_MAX grid rows"
    assert rows % NA_ROWS_PER_STEP == 0
    tq = NA_ROWS_PER_STEP * GRID_W
    return pl.pallas_call(
        functools.partial(_na_kernel, rows=rows, kh=kh),
        grid=(b, rows // NA_ROWS_PER_STEP),
        in_specs=[pl.BlockSpec((1, tq, width), lambda i, r: (i, r, col_q // width)),
                  pl.BlockSpec((1, n, width), lambda i, r: (i, 0, col_k // width)),
                  pl.BlockSpec((1, n, width), lambda i, r: (i, 0, col_v // width)),
                  pl.BlockSpec((1, n_ctx, width), lambda i, r: (i, 0, col_k // width)),
                  pl.BlockSpec((1, n_ctx, width), lambda i, r: (i, 0, col_v // width)),
                  pl.BlockSpec(bias.shape, lambda i, r: (0, 0, 0, 0))],
        out_specs=pl.BlockSpec((1, tq, width), lambda i, r: (i, r, 0)),
        out_shape=jax.ShapeDtypeStruct((b, n, width), BF16),
        compiler_params=_cparams(("parallel", "arbitrary")),
        name="na_mixer",
    )(px, px, px, pc, pc, bias)


def _merge_kernel(x_ref, y_ref, o_ref, ga_ref, gb_ref, gate_ref, shift_ref, scale_ref, g2_ref,
                  wglu_ref, bglu_ref, wa_ref, wb_ref, wo_ref, x1_ref, h2t_ref):
    z = _gelu(y_ref[0].astype(F32))
    gl = jnp.dot(z.astype(BF16), wglu_ref[...], preferred_element_type=F32) + bglu_ref[...]
    zz = z * _sigmoid(gl)
    ba = jnp.dot(zz.astype(BF16), wa_ref[...], preferred_element_type=F32)
    bb = jnp.dot(o_ref[0], wb_ref[...], preferred_element_type=F32)
    merged = _sigmoid(ga_ref[0].astype(F32)) * ba + _sigmoid(gb_ref[0].astype(F32)) * bb
    x1 = x_ref[0] + gate_ref[0] * jnp.dot(merged.astype(BF16), wo_ref[...], preferred_element_type=F32)
    x1_ref[0] = x1
    h2 = _rms(x1, g2_ref[...]) * (1.0 + scale_ref[0]) + shift_ref[0]
    h2t_ref[...] = h2.T.astype(h2t_ref.dtype)


def _merge(x, y_ssm, o_na, px, gate1, shift2, scale2, g2, w_glu, b_glu, w_a, w_b, w_o, col_ga, col_gb, tm):
    b, n, d = x.shape
    w = y_ssm.shape[2]
    tok = lambda i, j: (i, j, 0)
    per_b = lambda i, j: (i, 0, 0)
    full = lambda i, j: (0, 0)
    return pl.pallas_call(
        _merge_kernel,
        grid=(b, n // tm),
        in_specs=[pl.BlockSpec((1, tm, d), tok),
                  pl.BlockSpec((1, tm, w), tok),
                  pl.BlockSpec((1, tm, w), tok),
                  pl.BlockSpec((1, tm, d), lambda i, j: (i, j, col_ga // d)),
                  pl.BlockSpec((1, tm, d), lambda i, j: (i, j, col_gb // d)),
                  pl.BlockSpec((1, 1, d), per_b),
                  pl.BlockSpec((1, 1, d), per_b),
                  pl.BlockSpec((1, 1, d), per_b),
                  pl.BlockSpec((1, d), full),
                  pl.BlockSpec((w, w), full),
                  pl.BlockSpec((1, w), full),
                  pl.BlockSpec((w, d), full),
                  pl.BlockSpec((w, d), full),
                  pl.BlockSpec((d, d), full)],
        out_specs=[pl.BlockSpec((1, tm, d), tok),
                   pl.BlockSpec((d, tm), lambda i, j: (0, i * (n // tm) + j))],
        out_shape=[jax.ShapeDtypeStruct((b, n, d), F32), jax.ShapeDtypeStruct((d, b * n), BF16)],
        compiler_params=_cparams(("parallel", "parallel")),
        name="merge",
    )(x, y_ssm, o_na, px, px, gate1, shift2, scale2, g2, w_glu, b_glu, w_a, w_b, w_o)


CAND_B = (16, 8, 5, 4, 3, 2, 2, 2)


def _top16_exact(s):
    n = s.shape[0]
    iota = lax.broadcasted_iota(jnp.int32, s.shape, 0).astype(F32)
    cur = s
    rank = jnp.full(s.shape, 127.0, F32)
    vals = []
    for k in range(PEER_TOPK):
        m = jnp.max(cur, axis=0, keepdims=True)
        idx = jnp.min(jnp.where(cur == m, iota, float(n)), axis=0, keepdims=True)
        hit = iota == idx
        rank = jnp.where(hit, float(k), rank)
        cur = jnp.where(hit, -jnp.inf, cur)
        vals.append(m)
    return rank, jnp.concatenate(vals, axis=0)


def _top16_fast(s):
    cur = s
    vals = []
    for k in range(PEER_TOPK):
        m = jnp.max(cur, axis=0, keepdims=True)
        cur = jnp.where(cur == m, -MARK * (PEER_TOPK + k), cur)
        vals.append(m)
    marked = cur <= -MARK * PEER_TOPK
    rank = jnp.where(marked, cur * (-1.0 / MARK) - float(PEER_TOPK), 127.0)
    count = jnp.sum(jnp.where(marked, 1.0, 0.0), axis=0, keepdims=True)
    return rank, jnp.concatenate(vals, axis=0), count


def _cand_groups(v0, v1, combine, pad):
    sub = lax.broadcasted_iota(jnp.int32, (SUBLANES, v0.shape[1]), 0)
    first = combine(v0[0:1], v1)
    groups = [first[:SUBLANES], first[SUBLANES:]]
    for a in range(1, len(CAND_B)):
        piece = combine(v0[a:a + 1], v1[0:SUBLANES])
        groups.append(jnp.where(sub < CAND_B[a], piece, pad))
    groups.append(combine(v0[SUBLANES:], v1[0:1]))
    return groups


def _cand_list():
    out = [(b, 0, b) for b in range(CAND_B[0])]
    for a in range(1, len(CAND_B)):
        out += [(PEER_TOPK + SUBLANES * (a - 1) + b, a, b) for b in range(CAND_B[a])]
    base = PEER_TOPK + SUBLANES * (len(CAND_B) - 1)
    out += [(base + a - SUBLANES, a, 0) for a in range(SUBLANES, PEER_TOPK)]
    return out


def _select16(v0, v1):
    groups = _cand_groups(v0, v1, lambda x, y: x + y, -jnp.inf)
    sub = lax.broadcasted_iota(jnp.int32, groups[0].shape, 0)
    beaten = [jnp.zeros(groups[0].shape, F32) for _ in groups]
    for row, a, b in _cand_list():
        vc = v0[a:a + 1] + v1[b:b + 1]
        gc, rc = divmod(row, SUBLANES)
        for gi, grp in enumerate(groups):
            if gi == gc:
                first = jnp.where(sub > rc, jnp.where(vc >= grp, 1.0, 0.0), jnp.where(vc > grp, 1.0, 0.0))
                beaten[gi] = beaten[gi] + first
            else:
                first = (vc > grp) if gi < gc else (vc >= grp)
                beaten[gi] = jnp.where(first, beaten[gi] + 1.0, beaten[gi])
    return [jnp.where(bt < float(PEER_TOPK), 1.0, 0.0) for bt in beaten]


def _select16_fast(v0, v1):
    groups = _cand_groups(v0, v1, lambda x, y: x + y, -jnp.inf)
    cur = groups
    thr = None
    for _ in range(PEER_TOPK):
        m = cur[0]
        for grp in cur[1:]:
            m = jnp.maximum(m, grp)
        thr = jnp.max(m, axis=0, keepdims=True)
        cur = [jnp.where(grp == thr, -jnp.inf, grp) for grp in cur]
    sel = [jnp.where(grp >= thr, 1.0, 0.0) for grp in groups]
    total = sel[0]
    for sg in sel[1:]:
        total = total + sg
    return sel, jnp.sum(total, axis=0, keepdims=True)


def _route_head(st_ref, re_ref, nc_ref, lt, hd, exact):
    s0 = st_ref[lt, 2 * hd]
    s1 = st_ref[lt, 2 * hd + 1]
    worst = None
    if exact:
        (rank0, v0), (rank1, v1) = _top16_exact(s0), _top16_exact(s1)
        sel = _select16(v0, v1)
    else:
        rank0, v0, c0 = _top16_fast(s0)
        rank1, v1, c1 = _top16_fast(s1)
        sel, c2 = _select16_fast(v0, v1)
        worst = jnp.maximum(jnp.maximum(jnp.abs(c0 - float(PEER_TOPK)), jnp.abs(c1 - float(PEER_TOPK))),
                            jnp.abs(c2 - float(PEER_TOPK)))
    e0 = jnp.exp(v0 - v0[0:1])
    e1 = jnp.exp(v1 - v1[0:1])
    prod = _cand_groups(e0, e1, lambda x, y: x * y, 0.0)
    zsum = None
    for sg, pg in zip(sel, prod):
        part = jnp.sum(sg * pg, axis=0, keepdims=True)
        zsum = part if zsum is None else zsum + part
    cnt = jnp.zeros(rank0.shape, F32)
    for a in range(PEER_TOPK):
        if a == 0:
            n_a = jnp.sum(sel[0] + sel[1], axis=0, keepdims=True)
        elif a < SUBLANES:
            n_a = jnp.sum(sel[1 + a], axis=0, keepdims=True)
        else:
            n_a = sel[-1][a - SUBLANES:a - SUBLANES + 1]
        cnt = jnp.where(rank0 == float(a), n_a, cnt)
    coef = jnp.where(rank0 < float(PEER_TOPK), jnp.exp(s0 - v0[0:1]) / zsum, 0.0)
    e1f = jnp.where(rank1 < float(PEER_TOPK), jnp.exp(s1 - v1[0:1]), 0.0)
    re_ref[0, hd] = rank1.astype(re_ref.dtype)
    re_ref[1, hd] = e1f.astype(re_ref.dtype)
    nc_ref[0, hd, 0] = cnt
    nc_ref[1, hd, 0] = coef
    return worst


def _route_kernel(ht_ref, wq_ref, sk_ref, re_ref, nc_ref, st_ref, *, heads):
    lt = pl.program_id(1)
    kq = sk_ref.shape[2]

    @pl.when(lt == 0)
    def _():
        qt = jnp.dot(wq_ref[...], ht_ref[...], preferred_element_type=F32).astype(BF16)
        for hn in range(2 * heads):
            st = jnp.dot(sk_ref[hn % 2], qt[hn * kq:(hn + 1) * kq], preferred_element_type=F32)
            for ti in range(st_ref.shape[0]):
                st_ref[ti, hn] = st[:, ti * LANES:(ti + 1) * LANES]

    group = ROUTE_HEADS_PER_ITER

    def group_body(gi, carry):
        worst = None
        for k in range(group):
            dev = _route_head(st_ref, re_ref, nc_ref, lt, gi * group + k, exact=False)
            worst = dev if worst is None else jnp.maximum(worst, dev)

        @pl.when(jnp.max(worst) > 0.0)
        def _():
            for k in range(group):
                _route_head(st_ref, re_ref, nc_ref, lt, gi * group + k, exact=True)

        return carry

    lax.fori_loop(0, heads // group, group_body, 0)


def _route(h2t, wq_t, subkeys, heads, tm):
    d, t = h2t.shape
    keys = subkeys.shape[1]
    n_lt = tm // LANES
    assert keys == LANES
    return pl.pallas_call(
        functools.partial(_route_kernel, heads=heads),
        grid=(t // tm, n_lt),
        in_specs=[pl.BlockSpec((d, tm), lambda i, l: (0, i)),
                  pl.BlockSpec(wq_t.shape, lambda i, l: (0, 0)),
                  pl.BlockSpec(subkeys.shape, lambda i, l: (0, 0, 0))],
        out_specs=[pl.BlockSpec((2, heads, keys, LANES), lambda i, l: (0, 0, 0, i * n_lt + l)),
                   pl.BlockSpec((2, heads, 1, keys, LANES), lambda i, l: (0, 0, i * n_lt + l, 0, 0))],
        out_shape=[jax.ShapeDtypeStruct((2, heads, keys, t), BF16),
                   jax.ShapeDtypeStruct((2, heads, t // LANES, keys, LANES), F32)],
        scratch_shapes=[pltpu.VMEM((n_lt, 2 * heads, keys, LANES), F32)],
        compiler_params=_cparams(("parallel", "arbitrary")),
        name="peer_route",
    )(h2t, wq_t, subkeys)


def _peer_kernel(ht_ref, u_ref, vt_ref, re_ref, nc_ref, x1_ref, gate_ref, fg_ref, o_ref,
                 acc_ref, a_ref, w_ref, *, heads):
    j = pl.program_id(1)
    pack = 2 * SUBLANES

    @pl.when(j == 0)
    def _():
        acc_ref[...] = jnp.zeros_like(acc_ref)

    te, tm = a_ref.shape
    wc = PEER_COLS
    for r0 in range(0, te, PEER_SUB):
        rows = slice(r0, r0 + PEER_SUB)
        a_ref[rows, :] = jnp.dot(u_ref[rows, :], ht_ref[...], preferred_element_type=F32)
        for sl in range(r0 // LANES, (r0 + PEER_SUB) // LANES):
            rs = slice(sl * LANES, (sl + 1) * LANES)
            for c0 in range(0, tm, wc):
                cs = slice(c0, c0 + wc)
                tiles = range(c0 // LANES, (c0 + wc) // LANES)
                g = None
                for hd in range(heads):
                    nrow = jnp.concatenate([jnp.broadcast_to(nc_ref[0, hd, ti, sl:sl + 1, :], (pack, LANES))
                                            for ti in tiles], axis=1).astype(BF16)
                    crow = jnp.concatenate([jnp.broadcast_to(nc_ref[1, hd, ti, sl:sl + 1, :], (pack, LANES))
                                            for ti in tiles], axis=1).astype(BF16)
                    r1 = re_ref[0, hd, :, cs].reshape(LANES // pack, pack, wc)
                    e1 = re_ref[1, hd, :, cs].reshape(LANES // pack, pack, wc)
                    term = jnp.where(r1 < nrow[None], e1, jnp.zeros_like(e1)) * crow[None]
                    g = term if g is None else g + term
                w_ref[rs, cs] = _gelu(a_ref[rs, cs].astype(BF16)) * g.reshape(LANES, wc)
        acc_ref[...] += lax.dot_general(vt_ref[rows, :], w_ref[rows, :], (((0,), (0,)), ((), ())),
                                        preferred_element_type=F32)

    @pl.when(j == pl.num_programs(1) - 1)
    def _():
        x2 = x1_ref[...] + gate_ref[0] * acc_ref[...].T
        o_ref[...] = _rms(x2, fg_ref[...])


def _peer(h2t, u_tab, v_t, route_re, route_nc, x1, gate2, final_g, n_per_batch, tm, te):
    d, t = h2t.shape
    e = u_tab.shape[0]
    heads, keys = route_re.shape[1], route_re.shape[2]
    assert keys == LANES and te % (SUBLANES * LANES) == 0 and n_per_batch % tm == 0
    return pl.pallas_call(
        functools.partial(_peer_kernel, heads=heads),
        grid=(t // tm, e // te),
        in_specs=[pl.BlockSpec((d, tm), lambda i, j: (0, i)),
                  pl.BlockSpec((te, d), lambda i, j: (j, 0)),
                  pl.BlockSpec((te, d), lambda i, j: (j, 0)),
                  pl.BlockSpec((2, heads, keys, tm), lambda i, j: (0, 0, 0, i)),
                  pl.BlockSpec((2, heads, tm // LANES, te // LANES, LANES), lambda i, j: (0, 0, i, j, 0)),
                  pl.BlockSpec((tm, d), lambda i, j: (i, 0)),
                  pl.BlockSpec((1, 1, d), lambda i, j: ((i * tm) // n_per_batch, 0, 0)),
                  pl.BlockSpec((1, d), lambda i, j: (0, 0))],
        out_specs=pl.BlockSpec((tm, d), lambda i, j: (i, 0)),
        out_shape=jax.ShapeDtypeStruct((t, d), F32),
        scratch_shapes=[pltpu.VMEM((d, tm), F32),
                        pltpu.VMEM((te, tm), F32),
                        pltpu.VMEM((te, tm), BF16)],
        compiler_params=_cparams(("parallel", "arbitrary")),
        name="peer_dense",
    )(h2t, u_tab, v_t, route_re, route_nc, x1, gate2, final_g)


def _layer(x, ctx, c, c_ctx, p, final_g):
    b, n, d = x.shape
    n_ctx = ctx.shape[1]
    width = p['w_glu'].shape[0]
    col_k, col_v, col_q = width, 2 * width, 3 * width
    col_ga, col_gb = 4 * width, 4 * width + d
    heads_p = p['peer_w_q'].shape[1] // (2 * p['peer_subkeys'].shape[2])
    assert b % SUBLANES == 0 and n % (GRID_W * SSM_CHUNK) == 0 and n_ctx % SSM_CHUNK == 0
    assert width % LANES == 0 and col_ga % d == 0

    rows = 2 * SUBLANES
    cond = jnp.zeros((rows, d), F32).at[:b].set(c).at[b].set(c_ctx)
    mod = _adaln(cond, p['w_mod'], p['b_mod'])
    chunks = [mod[:, i * d:(i + 1) * d] for i in range(6)]
    shift1, scale1, gate1, shift2, scale2, gate2 = [m[:b, None, :] for m in chunks]
    shift_c, scale_c = chunks[0][b:b + 1, None, :], chunks[1][b:b + 1, None, :]

    w_in = p['w_in'].astype(BF16)
    g1 = p['norm1_g'].reshape(1, d)
    px = _inproj(x, g1, shift1, scale1, w_in, tm=256)
    pc = _inproj(ctx, g1, shift_c, scale_c, w_in[:, :col_q], tm=n_ctx)

    w_all, m_out, a_tab = _s5_tables(p['ssm_a_re'], p['ssm_a_im'], p['ssm_log_dt'], p['ssm_b_re'],
                                     p['ssm_b_im'], p['ssm_c_re'], p['ssm_c_im'], p['ssm_d'])
    cc, cl = n_ctx // SSM_CHUNK, n // SSM_CHUNK
    u_all = jnp.concatenate([_to_groups(pc[..., :width]), _to_groups(px[..., :width])], axis=1)
    groups = u_all.shape[0]
    u_all = u_all.reshape(groups, (cc + cl) * b, SSM_CHUNK * SSM_GROUP)
    y_g = _s5(u_all, w_all, m_out, a_tab, b, cc, cl)
    y_ssm = (y_g.reshape(groups, cl, b, SSM_CHUNK, SSM_GROUP)
             .transpose(2, 1, 3, 0, 4).reshape(b, n, width))

    o_na = _na(px, pc, _na_bias(p['na_rpb']), width, col_k, col_v, col_q)

    x1, h2t = _merge(x, y_ssm, o_na, px, gate1, shift2, scale2, p['norm2_g'].reshape(1, d),
                     p['w_glu'].astype(BF16), p['b_glu'].reshape(1, width),
                     p['w_branch_a'].astype(BF16), p['w_branch_b'].astype(BF16),
                     p['w_out'].astype(BF16), col_ga, col_gb, tm=256)

    t = b * n
    route_re, route_nc = _route(h2t, p['peer_w_q'].T.astype(BF16), p['peer_subkeys'].astype(BF16),
                                heads_p, tm=512)
    out = _peer(h2t, p['peer_u'].astype(BF16), p['peer_v'].astype(BF16), route_re, route_nc,
                x1.reshape(t, d), gate2, final_g.reshape(1, d), n, tm=512, te=2048)
    return out.reshape(b, n, d)


def kernel(x, c, ctx, c_ctx, w_mod, b_mod, norm1_g, norm2_g, w_in, ssm_a_re, ssm_a_im, ssm_log_dt,
           ssm_b_re, ssm_b_im, ssm_c_re, ssm_c_im, ssm_d, w_glu, b_glu, w_branch_a, w_branch_b, na_rpb,
           w_out, peer_w_q, peer_subkeys, peer_u, peer_v, final_g):
    assert w_mod.shape[0] == 1, "single-layer stack"
    p = {
        'w_mod': w_mod[0], 'b_mod': b_mod[0], 'norm1_g': norm1_g[0], 'norm2_g': norm2_g[0],
        'w_in': w_in[0],
        'ssm_a_re': ssm_a_re[0], 'ssm_a_im': ssm_a_im[0], 'ssm_log_dt': ssm_log_dt[0],
        'ssm_b_re': ssm_b_re[0], 'ssm_b_im': ssm_b_im[0], 'ssm_c_re': ssm_c_re[0], 'ssm_c_im': ssm_c_im[0],
        'ssm_d': ssm_d[0], 'w_glu': w_glu[0], 'b_glu': b_glu[0],
        'w_branch_a': w_branch_a[0], 'w_branch_b': w_branch_b[0], 'na_rpb': na_rpb[0], 'w_out': w_out[0],
        'peer_w_q': peer_w_q[0], 'peer_subkeys': peer_subkeys[0], 'peer_u': peer_u[0], 'peer_v': peer_v[0],
    }
    return _layer(x, ctx, c, c_ctx, p, final_g)
```
